```python
import math
import jax, jax.numpy as jnp
from jax import lax
import numpy as np

D_MODEL = 2048
BATCH = 2
SEQ = 8192
DEPTH = 4

N_MIXERS = 3
N_MLA = (DEPTH + 2) // 3
N_RWKV = (DEPTH + 1) // 3
N_SWA = DEPTH // 3

DEEPNORM_ALPHA = (2.0 * DEPTH) ** 0.25
DEEPNORM_BETA = (8.0 * DEPTH) ** -0.25
LN_EPS = 1e-5
RMS_EPS = 1e-6
NEG_INF = -1e30
Q_BLOCK = 128

MLA_HEADS = 16
MLA_Q_RANK = 512
MLA_KV_RANK = 512
MLA_NOPE = 128
MLA_ROPE = 64
MLA_V = 128
ROPE_THETA = 10000.0

RWKV_HEAD = 64
RWKV_HEADS = D_MODEL // RWKV_HEAD
RWKV_DECAY_LORA = 96
RWKV_AAA_LORA = 96
RWKV_GATE_LORA = 256
RWKV_GN_EPS = 64e-5

SWA_HEAD = 64
SWA_Q_HEADS = D_MODEL // SWA_HEAD
SWA_KV_HEADS = 4
SWA_GROUP = SWA_Q_HEADS // SWA_KV_HEADS
WINDOW = 128
REL_BUCKETS = 32
REL_MAX_DIST = WINDOW

MLP_HIDDEN = 4 * D_MODEL

kernel_name = "hybrid_mla_rwkv7_swa_deepnorm"


def layer_norm(x, g, b):
    xf = x.astype(jnp.float32)
    mu = jnp.mean(xf, -1, keepdims=True)
    var = jnp.mean(jnp.square(xf - mu), -1, keepdims=True)
    return ((xf - mu) * lax.rsqrt(var + LN_EPS) * g + b).astype(x.dtype)


def rms_norm(x, g):
    xf = x.astype(jnp.float32)
    return (xf * lax.rsqrt(jnp.mean(jnp.square(xf), -1, keepdims=True) + RMS_EPS) * g).astype(x.dtype)


def rope(x, pos):
    half = x.shape[-1] // 2
    inv = ROPE_THETA ** (-jnp.arange(half, dtype=jnp.float32) / half)
    ang = pos.astype(jnp.float32)[:, None] * inv[None, :]
    cos = jnp.cos(ang)[:, None, :]
    sin = jnp.sin(ang)[:, None, :]
    x1 = x[..., :half].astype(jnp.float32)
    x2 = x[..., half:].astype(jnp.float32)
    return jnp.concatenate([x1 * cos - x2 * sin, x2 * cos + x1 * sin], -1).astype(x.dtype)


def causal_block_attention(q, k, v, scale):
    B, S, H, Dq = q.shape
    nb = S // Q_BLOCK
    q_blocks = q.reshape(B, nb, Q_BLOCK, H, Dq).swapaxes(0, 1)
    k_pos = jnp.arange(S)

    def one_block(args):
        qb, start = args
        s = jnp.einsum('bqhd,bkhd->bhqk', qb, k).astype(jnp.float32) * scale
        q_pos = start + jnp.arange(Q_BLOCK)
        s = jnp.where(k_pos[None, :] <= q_pos[:, None], s, NEG_INF)
        p = jax.nn.softmax(s, axis=-1).astype(v.dtype)
        return jnp.einsum('bhqk,bkhd->bqhd', p, v)

    o = lax.map(one_block, (q_blocks, jnp.arange(nb) * Q_BLOCK))
    return o.swapaxes(0, 1).reshape(B, S, H, v.shape[-1])


def mla_mixer(x, w_in, q_norm, kv_norm, w_q_b, w_kv_b, w_out):
    B, S, _ = x.shape
    h = x @ w_in
    q_lat, kv_lat, k_rope = jnp.split(h, [MLA_Q_RANK, MLA_Q_RANK + MLA_KV_RANK], axis=-1)
    q = (rms_norm(q_lat, q_norm) @ w_q_b).reshape(B, S, MLA_HEADS, MLA_NOPE + MLA_ROPE)
    kv = (rms_norm(kv_lat, kv_norm) @ w_kv_b).reshape(B, S, MLA_HEADS, MLA_NOPE + MLA_V)
    pos = jnp.arange(S)
    q = jnp.concatenate([q[..., :MLA_NOPE], rope(q[..., MLA_NOPE:], pos)], -1)
    k_rope = rope(k_rope[:, :, None, :], pos)
    k = jnp.concatenate([kv[..., :MLA_NOPE],
                         jnp.broadcast_to(k_rope, (B, S, MLA_HEADS, MLA_ROPE))], -1)
    v = kv[..., MLA_NOPE:]
    o = causal_block_attention(q, k, v, (MLA_NOPE + MLA_ROPE) ** -0.5)
    return o.reshape(B, S, MLA_HEADS * MLA_V) @ w_out


def wkv7_scan(r, w, k, v, a, b):
    B, S, H, N = r.shape
    seq_first = lambda t: jnp.moveaxis(t.astype(jnp.float32), 1, 0)

    def step(state, inp):
        rt, wt, kt, vt, at, bt = inp
        sa = jnp.einsum('bhvk,bhk->bhv', state, at)
        state = (state * wt[:, :, None, :] + sa[..., None] * bt[:, :, None, :]
                 + vt[..., None] * kt[:, :, None, :])
        return state, jnp.einsum('bhvk,bhk->bhv', state, rt)

    state0 = jnp.zeros((B, H, N, N), jnp.float32)
    _, y = lax.scan(step, state0, tuple(seq_first(t) for t in (r, w, k, v, a, b)))
    return jnp.moveaxis(y, 0, 1)


def rwkv7_mixer(x, mix, w_in, w0, w1, w2, a0, a1, a2, g1, g2, k_k, k_a, r_k, ln_g, ln_b, w_out):
    B, S, D = x.shape
    H, N = RWKV_HEADS, RWKV_HEAD
    heads = lambda t: t.reshape(B, S, H, N)
    xx = jnp.pad(x, ((0, 0), (1, 0), (0, 0)))[:, :-1] - x
    x_rkv = x[None] + xx[None] * mix[jnp.array([0, 2, 3])][:, None, None, :]
    xw = x + xx * mix[1]
    xa = x + xx * mix[4]
    xg = x + xx * mix[5]
    r, k, v = jnp.einsum('nbsd,nde->nbse', x_rkv, w_in)
    log_w = -jax.nn.softplus(-(w0 + jnp.tanh(xw @ w1) @ w2)) - 0.5
    decay = jnp.exp(-jnp.exp(log_w.astype(jnp.float32)))
    a = jax.nn.sigmoid(a0 + (xa @ a1) @ a2)
    g = jax.nn.sigmoid(xg @ g1) @ g2
    kk = heads(k * k_k).astype(jnp.float32)
    kk = kk / jnp.maximum(jnp.sqrt(jnp.sum(jnp.square(kk), -1, keepdims=True)), 1e-12)
    k = k * (1.0 + (a - 1.0) * k_a)
    y = wkv7_scan(heads(r), heads(decay), heads(k), heads(v), -kk, kk * heads(a).astype(jnp.float32))
    mu = jnp.mean(y, -1, keepdims=True)
    var = jnp.mean(jnp.square(y - mu), -1, keepdims=True)
    y = ((y - mu) * lax.rsqrt(var + RWKV_GN_EPS)).reshape(B, S, D) * ln_g + ln_b
    bonus = jnp.sum(heads(r) * heads(k) * r_k, -1, keepdims=True) * heads(v)
    y = y + bonus.reshape(B, S, D)
    return (y * g).astype(x.dtype) @ w_out


def t5_bucket(dist):
    max_exact = REL_BUCKETS // 2
    n = jnp.maximum(dist, 0)
    nf = jnp.maximum(n, 1).astype(jnp.float32)
    large = max_exact + (jnp.log(nf / max_exact) / math.log(REL_MAX_DIST / max_exact)
                         * (REL_BUCKETS - max_exact)).astype(jnp.int32)
    large = jnp.minimum(large, REL_BUCKETS - 1)
    return jnp.where(n < max_exact, n, large)


def swa_mixer(x, w_in, b_in, sinks, w_out, b_out, rel_bias):
    B, S, _ = x.shape
    Hq, Hk, G, Dh, W = SWA_Q_HEADS, SWA_KV_HEADS, SWA_GROUP, SWA_HEAD, WINDOW
    nb = S // W
    qkv = x @ w_in + b_in
    q, k, v = jnp.split(qkv, [Hq * Dh, (Hq + Hk) * Dh], axis=-1)
    q = q.reshape(B, nb, W, Hk, G, Dh)
    k = k.reshape(B, nb, W, Hk, Dh)
    v = v.reshape(B, nb, W, Hk, Dh)
    band = lambda t: jnp.concatenate(
        [jnp.pad(t, ((0, 0), (1, 0), (0, 0), (0, 0), (0, 0)))[:, :-1], t], axis=2)
    k_band, v_band = band(k), band(v)
    dist = jnp.arange(W)[:, None] + W - jnp.arange(2 * W)[None, :]
    in_window = (dist >= 0) & (dist < W)
    bias = rel_bias[t5_bucket(dist)].astype(jnp.float32)
    bias = jnp.where(in_window[..., None], bias, NEG_INF)
    bias = bias.transpose(2, 0, 1).reshape(Hk, G, W, 2 * W)
    sink = sinks.astype(jnp.float32).reshape(Hk, G, 1, 1)
    scale = Dh ** -0.5

    def one_block(args):
        qb, kb, vb, blk = args
        s = jnp.einsum('bqhgd,bshd->bhgqs', qb, kb).astype(jnp.float32) * scale + bias
        key_ok = blk * W + jnp.arange(2 * W) - W >= 0
        s = jnp.where(key_ok, s, NEG_INF)
        s = jnp.concatenate([s, jnp.broadcast_to(sink, s.shape[:-1] + (1,))], -1)
        p = jax.nn.softmax(s, axis=-1)[..., :-1].astype(vb.dtype)
        return jnp.einsum('bhgqs,bshd->bqhgd', p, vb)

    o = lax.map(one_block, (q.swapaxes(0, 1), k_band.swapaxes(0, 1), v_band.swapaxes(0, 1), jnp.arange(nb)))
    o = o.swapaxes(0, 1).reshape(B, S, Hq * Dh)
    return o @ w_out + b_out


def setup_inputs(seed: int = 0) -> dict:
    keys = iter(jax.random.split(jax.random.key(seed), 48))
    nrm = lambda shape, scale: jax.random.normal(next(keys), shape, jnp.float32) * scale
    D = D_MODEL
    mla_in = MLA_Q_RANK + MLA_KV_RANK + MLA_ROPE
    swa_in = (SWA_Q_HEADS + 2 * SWA_KV_HEADS) * SWA_HEAD
    return {
        "x": nrm((BATCH, SEQ, D), 1.0),
        "mla_w_in": nrm((N_MLA, D, mla_in), D ** -0.5),
        "mla_q_norm": 1.0 + nrm((N_MLA, MLA_Q_RANK), 0.02),
        "mla_kv_norm": 1.0 + nrm((N_MLA, MLA_KV_RANK), 0.02),
        "mla_w_q_b": nrm((N_MLA, MLA_Q_RANK, MLA_HEADS * (MLA_NOPE + MLA_ROPE)), MLA_Q_RANK ** -0.5),
        "mla_w_kv_b": nrm((N_MLA, MLA_KV_RANK, MLA_HEADS * (MLA_NOPE + MLA_V)), MLA_KV_RANK ** -0.5),
        "mla_w_out": nrm((N_MLA, MLA_HEADS * MLA_V, D), DEEPNORM_BETA * (MLA_HEADS * MLA_V) ** -0.5),
        "rwkv_mix": jax.random.uniform(next(keys), (N_RWKV, 6, D), jnp.float32),
        "rwkv_w_in": nrm((N_RWKV, 3, D, D), D ** -0.5),
        "rwkv_w0": jax.random.uniform(next(keys), (N_RWKV, D), jnp.float32, -6.0, -1.0),
        "rwkv_w1": nrm((N_RWKV, D, RWKV_DECAY_LORA), D ** -0.5),
        "rwkv_w2": nrm((N_RWKV, RWKV_DECAY_LORA, D), 0.5 * RWKV_DECAY_LORA ** -0.5),
        "rwkv_a0": nrm((N_RWKV, D), 0.02),
        "rwkv_a1": nrm((N_RWKV, D, RWKV_AAA_LORA), D ** -0.5),
        "rwkv_a2": nrm((N_RWKV, RWKV_AAA_LORA, D), 0.1 * RWKV_AAA_LORA ** -0.5),
        "rwkv_g1": nrm((N_RWKV, D, RWKV_GATE_LORA), D ** -0.5),
        "rwkv_g2": nrm((N_RWKV, RWKV_GATE_LORA, D), RWKV_GATE_LORA ** -0.5),
        "rwkv_k_k": 0.85 + nrm((N_RWKV, D), 0.02),
        "rwkv_k_a": 1.0 + nrm((N_RWKV, D), 0.02),
        "rwkv_r_k": nrm((N_RWKV, RWKV_HEADS, RWKV_HEAD), 0.1),
        "rwkv_ln_g": 1.0 + nrm((N_RWKV, D), 0.02),
        "rwkv_ln_b": nrm((N_RWKV, D), 0.02),
        "rwkv_w_out": nrm((N_RWKV, D, D), DEEPNORM_BETA * D ** -0.5),
        "swa_w_in": nrm((N_SWA, D, swa_in), D ** -0.5),
        "swa_b_in": nrm((N_SWA, swa_in), 0.02),
        "swa_sinks": nrm((N_SWA, SWA_Q_HEADS), 1.0),
        "swa_w_out": nrm((N_SWA, SWA_Q_HEADS * SWA_HEAD, D), DEEPNORM_BETA * (SWA_Q_HEADS * SWA_HEAD) ** -0.5),
        "swa_b_out": nrm((N_SWA, D), 0.02),
        "rel_bias": nrm((REL_BUCKETS, SWA_Q_HEADS), 0.5),
        "ln_g": 1.0 + nrm((DEPTH, 2, D), 0.02),
        "ln_b": nrm((DEPTH, 2, D), 0.02),
        "mlp_up": nrm((DEPTH, D, MLP_HIDDEN), D ** -0.5),
        "mlp_down": nrm((DEPTH, MLP_HIDDEN, D), DEEPNORM_BETA * MLP_HIDDEN ** -0.5),
    }


def reference(x, mla_w_in, mla_q_norm, mla_kv_norm, mla_w_q_b, mla_w_kv_b, mla_w_out,
              rwkv_mix, rwkv_w_in, rwkv_w0, rwkv_w1, rwkv_w2, rwkv_a0, rwkv_a1, rwkv_a2,
              rwkv_g1, rwkv_g2, rwkv_k_k, rwkv_k_a, rwkv_r_k, rwkv_ln_g, rwkv_ln_b, rwkv_w_out,
              swa_w_in, swa_b_in, swa_sinks, swa_w_out, swa_b_out,
              rel_bias, ln_g, ln_b, mlp_up, mlp_down):
    h = x
    for i in range(DEPTH):
        j = i // N_MIXERS
        kind = i % N_MIXERS
        if kind == 0:
            mixed = mla_mixer(h, mla_w_in[j], mla_q_norm[j], mla_kv_norm[j],
                              mla_w_q_b[j], mla_w_kv_b[j], mla_w_out[j])
        elif kind == 1:
            mixed = rwkv7_mixer(h, rwkv_mix[j], rwkv_w_in[j], rwkv_w0[j], rwkv_w1[j], rwkv_w2[j],
                                rwkv_a0[j], rwkv_a1[j], rwkv_a2[j], rwkv_g1[j], rwkv_g2[j],
                                rwkv_k_k[j], rwkv_k_a[j], rwkv_r_k[j], rwkv_ln_g[j], rwkv_ln_b[j],
                                rwkv_w_out[j])
        else:
            mixed = swa_mixer(h, swa_w_in[j], swa_b_in[j], swa_sinks[j], swa_w_out[j],
                              swa_b_out[j], rel_bias)
        h = layer_norm(DEEPNORM_ALPHA * h + mixed, ln_g[i, 0], ln_b[i, 0])
        ff = jnp.square(jax.nn.relu(h @ mlp_up[i])) @ mlp_down[i]
        h = layer_norm(DEEPNORM_ALPHA * h + ff, ln_g[i, 1], ln_b[i, 1])
    return h
```

```python
import functools
import math

import jax
import jax.numpy as jnp
from jax import lax
from jax.experimental import pallas as pl
from jax.experimental.pallas import tpu as pltpu

F32 = jnp.float32
MXU_DTYPE = jnp.bfloat16

LN_EPS = 1e-5
RMS_EPS = 1e-6
NEG_INF = -1e30
ROPE_THETA = 10000.0

LANES = 128
SUBLANES = 8
VMEM_LIMIT = 56 * 1024 * 1024

MLA_NOPE = 128
MLA_ROPE = 64
MLA_V = 128
RWKV_HEAD = 64
RWKV_GN_EPS = 64e-5
RWKV_CHUNK = 64
SWA_HEAD = 64
SWA_WINDOW = 128
REL_BUCKETS = 32


def _params(*sem):
    return pltpu.CompilerParams(dimension_semantics=sem, vmem_limit_bytes=VMEM_LIMIT)


def _tile(n, want):
    t = min(n, want)
    assert n % t == 0, (n, t)
    return t


def _resident(shape):
    nd = len(shape)
    return pl.BlockSpec(shape, lambda *_: (0,) * nd, pipeline_mode=pl.Buffered(1))


def _mm(a, b):
    return jnp.dot(a.astype(MXU_DTYPE), b.astype(MXU_DTYPE), preferred_element_type=F32)


def _mm_nt(a, b):
    return lax.dot_general(a.astype(MXU_DTYPE), b.astype(MXU_DTYPE),
                           (((1,), (1,)), ((), ())), preferred_element_type=F32)


def _mm_tn(a, b):
    return lax.dot_general(a.astype(MXU_DTYPE), b.astype(MXU_DTYPE),
                           (((0,), (0,)), ((), ())), preferred_element_type=F32)


def _split(a):
    hi = a.astype(MXU_DTYPE)
    lo = (a - hi.astype(F32)).astype(MXU_DTYPE)
    return hi, lo


def _mm3(a, b):
    ah, al = _split(a)
    bh, bl = _split(b)
    return (jnp.dot(ah, bh, preferred_element_type=F32)
            + jnp.dot(ah, bl, preferred_element_type=F32)
            + jnp.dot(al, bh, preferred_element_type=F32))


def _layer_norm(y, g, b):
    mu = jnp.mean(y, -1, keepdims=True)
    d = y - mu
    var = jnp.mean(d * d, -1, keepdims=True)
    return d * lax.rsqrt(var + LN_EPS) * g + b


def _rms_norm(y, g):
    return y * lax.rsqrt(jnp.mean(y * y, -1, keepdims=True) + RMS_EPS) * g


def _sigmoid(z):
    return 1.0 / (1.0 + jnp.exp(-z))


def _proj_res_ln_kernel(a_ref, w_ref, bias_ref, x_ref, g_ref, b_ref, o_ref, *, alpha):
    y = jnp.dot(a_ref[...], w_ref[...], preferred_element_type=F32)
    y = y + bias_ref[...] + alpha * x_ref[...]
    o_ref[...] = _layer_norm(y, g_ref[...], b_ref[...])


def _proj_res_ln(a, w, bias, x, g, b, alpha):
    T, K = a.shape
    D = w.shape[1]
    tm = _tile(T, 512)
    return pl.pallas_call(
        functools.partial(_proj_res_ln_kernel, alpha=alpha),
        grid=(T // tm,),
        in_specs=[pl.BlockSpec((tm, K), lambda i: (i, 0)),
                  _resident((K, D)), _resident((1, D)),
                  pl.BlockSpec((tm, D), lambda i: (i, 0)),
                  _resident((1, D)), _resident((1, D))],
        out_specs=pl.BlockSpec((tm, D), lambda i: (i, 0)),
        out_shape=jax.ShapeDtypeStruct((T, D), F32),
        compiler_params=_params("parallel"),
        name="proj_res_ln",
    )(a, w, bias.reshape(1, D), x, g.reshape(1, D), b.reshape(1, D))


def _mlp_kernel(x_ref, wu_ref, wd_ref, g_ref, b_ref, o_ref, xb_ref, acc_ref, *, alpha):
    j = pl.program_id(1)

    @pl.when(j == 0)
    def _():
        x = x_ref[...]
        xb_ref[...] = x.astype(MXU_DTYPE)
        acc_ref[...] = alpha * x

    u = jnp.dot(xb_ref[...], wu_ref[...], preferred_element_type=F32)
    u = jnp.maximum(u, 0.0)
    u = (u * u).astype(MXU_DTYPE)
    acc_ref[...] += jnp.dot(u, wd_ref[...], preferred_element_type=F32)

    @pl.when(j == pl.num_programs(1) - 1)
    def _():
        o_ref[...] = _layer_norm(acc_ref[...], g_ref[...], b_ref[...])


def _mlp_res_ln(x, wu, wd, g, b, alpha):
    T, D = x.shape
    Hd = wu.shape[1]
    tm = _tile(T, 512)
    th = _tile(Hd, 1024)
    return pl.pallas_call(
        functools.partial(_mlp_kernel, alpha=alpha),
        grid=(T // tm, Hd // th),
        in_specs=[pl.BlockSpec((tm, D), lambda i, j: (i, 0)),
                  pl.BlockSpec((D, th), lambda i, j: (0, j)),
                  pl.BlockSpec((th, D), lambda i, j: (j, 0)),
                  _resident((1, D)), _resident((1, D))],
        out_specs=pl.BlockSpec((tm, D), lambda i, j: (i, 0)),
        out_shape=jax.ShapeDtypeStruct((T, D), F32),
        scratch_shapes=[pltpu.VMEM((tm, D), MXU_DTYPE), pltpu.VMEM((tm, D), F32)],
        compiler_params=_params("parallel", "arbitrary"),
        name="mlp_res_ln",
    )(x, wu, wd, g.reshape(1, D), b.reshape(1, D))


def _mla_proj_kernel(x_ref, wqa_ref, wkva_ref, wkr_ref, wkrr_ref, qg_ref, kvg_ref,
                     wqn_ref, wqr_ref, wqrr_ref, wkn_ref, wv_ref, cos_ref, sin_ref,
                     qn_ref, qr_ref, kn_ref, v_ref, kr_ref, *, heads, scale):
    xb = x_ref[...].astype(MXU_DTYPE)
    cos = cos_ref[...]
    sin = sin_ref[...]
    q_lat = jnp.dot(xb, wqa_ref[...], preferred_element_type=F32)
    kv_lat = jnp.dot(xb, wkva_ref[...], preferred_element_type=F32)
    kr = jnp.dot(xb, wkr_ref[...], preferred_element_type=F32)
    krr = jnp.dot(xb, wkrr_ref[...], preferred_element_type=F32)
    kr_ref[...] = (kr * cos + krr * sin).astype(kr_ref.dtype)

    qn = _rms_norm(q_lat, qg_ref[...]).astype(MXU_DTYPE)
    qn_ref[...] = (jnp.dot(qn, wqn_ref[...], preferred_element_type=F32) * scale).astype(qn_ref.dtype)
    qr = jnp.dot(qn, wqr_ref[...], preferred_element_type=F32)
    qrr = jnp.dot(qn, wqrr_ref[...], preferred_element_type=F32)
    for h in range(heads):
        sl = slice(h * LANES, (h + 1) * LANES)
        qr_ref[:, sl] = ((qr[:, sl] * cos + qrr[:, sl] * sin) * scale).astype(qr_ref.dtype)

    kvn = _rms_norm(kv_lat, kvg_ref[...]).astype(MXU_DTYPE)
    kn_ref[...] = jnp.dot(kvn, wkn_ref[...], preferred_element_type=F32).astype(kn_ref.dtype)
    v_ref[...] = jnp.dot(kvn, wv_ref[...], preferred_element_type=F32).astype(v_ref.dtype)


def _rot_half_cols(w):
    half = w.shape[-1] // 2
    return jnp.concatenate([-w[..., half:], w[..., :half]], -1)


def _pad_lanes(w, width):
    pad = [(0, 0)] * (w.ndim - 1) + [(0, width - w.shape[-1])]
    return jnp.pad(w, pad)


def _mla_project(x, w_in, q_norm, kv_norm, w_q_b, w_kv_b, S):
    T, D = x.shape
    q_rank = q_norm.shape[-1]
    kv_rank = kv_norm.shape[-1]
    H = w_q_b.shape[1] // (MLA_NOPE + MLA_ROPE)
    scale = (MLA_NOPE + MLA_ROPE) ** -0.5

    wqa = w_in[:, :q_rank].astype(MXU_DTYPE)
    wkva = w_in[:, q_rank:q_rank + kv_rank].astype(MXU_DTYPE)
    wkr_raw = w_in[:, q_rank + kv_rank:]
    wkr = _pad_lanes(wkr_raw, LANES).astype(MXU_DTYPE)
    wkrr = _pad_lanes(_rot_half_cols(wkr_raw), LANES).astype(MXU_DTYPE)
    wq3 = w_q_b.reshape(q_rank, H, MLA_NOPE + MLA_ROPE)
    wqn = wq3[:, :, :MLA_NOPE].reshape(q_rank, H * MLA_NOPE).astype(MXU_DTYPE)
    wq_rope = wq3[:, :, MLA_NOPE:]
    wqr = _pad_lanes(wq_rope, LANES).reshape(q_rank, H * LANES).astype(MXU_DTYPE)
    wqrr = _pad_lanes(_rot_half_cols(wq_rope), LANES).reshape(q_rank, H * LANES).astype(MXU_DTYPE)
    wkv3 = w_kv_b.reshape(kv_rank, H, MLA_NOPE + MLA_V)
    wkn = wkv3[:, :, :MLA_NOPE].reshape(kv_rank, H * MLA_NOPE).astype(MXU_DTYPE)
    wv = wkv3[:, :, MLA_NOPE:].reshape(kv_rank, H * MLA_V).astype(MXU_DTYPE)

    half = MLA_ROPE // 2
    inv = ROPE_THETA ** (-jnp.arange(half, dtype=F32) / half)
    ang = jnp.arange(S, dtype=F32)[:, None] * inv[None, :]
    cos = _pad_lanes(jnp.concatenate([jnp.cos(ang), jnp.cos(ang)], -1), LANES)
    sin = _pad_lanes(jnp.concatenate([jnp.sin(ang), jnp.sin(ang)], -1), LANES)

    tm = _tile(S, 512)
    ns = S // tm
    row = lambda i: (i, 0)
    pos = lambda i: (i % ns, 0)
    wide = jax.ShapeDtypeStruct((T, H * LANES), MXU_DTYPE)
    outs = pl.pallas_call(
        functools.partial(_mla_proj_kernel, heads=H, scale=scale),
        grid=(T // tm,),
        in_specs=[pl.BlockSpec((tm, D), row),
                  _resident(wqa.shape), _resident(wkva.shape), _resident(wkr.shape),
                  _resident(wkrr.shape), _resident((1, q_rank)), _resident((1, kv_rank)),
                  _resident(wqn.shape), _resident(wqr.shape), _resident(wqrr.shape),
                  _resident(wkn.shape), _resident(wv.shape),
                  pl.BlockSpec((tm, LANES), pos), pl.BlockSpec((tm, LANES), pos)],
        out_specs=[pl.BlockSpec((tm, H * LANES), row)] * 4 + [pl.BlockSpec((tm, LANES), row)],
        out_shape=[wide, wide, wide, wide, jax.ShapeDtypeStruct((T, LANES), MXU_DTYPE)],
        compiler_params=_params("parallel"),
        name="mla_proj",
    )(x, wqa, wkva, wkr, wkrr, q_norm.reshape(1, -1), kv_norm.reshape(1, -1),
      wqn, wqr, wqrr, wkn, wv, cos, sin)
    return outs, H


def _flash_kernel(qn_ref, qr_ref, kn_ref, kr_ref, v_ref, o_ref, *, tile):
    i = pl.program_id(2)
    q = jnp.concatenate([qn_ref[...], qr_ref[...]], -1)

    def step(j, carry, masked):
        m, l, acc = carry
        rows = pl.ds(pl.multiple_of(j * tile, tile), tile)
        k = jnp.concatenate([kn_ref[rows, :], kr_ref[rows, :]], -1)
        s = _mm_nt(q, k)
        if masked:
            r = lax.broadcasted_iota(jnp.int32, s.shape, 0)
            c = lax.broadcasted_iota(jnp.int32, s.shape, 1)
            s = jnp.where(c <= r, s, NEG_INF)
        m_new = jnp.maximum(m, jnp.max(s, -1, keepdims=True))
        p = jnp.exp(s - m_new)
        corr = jnp.exp(m - m_new)
        l = corr * l + jnp.sum(p, -1, keepdims=True)
        acc = corr * acc + jnp.dot(p.astype(MXU_DTYPE), v_ref[rows, :],
                                   preferred_element_type=F32)
        return m_new, l, acc

    init = (jnp.full((tile, 1), NEG_INF, F32), jnp.zeros((tile, 1), F32),
            jnp.zeros((tile, v_ref.shape[-1]), F32))
    carry = lax.fori_loop(0, i, lambda j, c: step(j, c, False), init)
    _, l, acc = step(i, carry, True)
    o_ref[...] = (acc / l).astype(o_ref.dtype)


def _flash_attention(qn, qr, kn, kr, v, B, S, H):
    T = B * S
    tile = _tile(S, 512)
    nq = S // tile
    qmap = lambda b, h, i: (b * nq + i, h)
    kmap = lambda b, h, i: (b, h)
    return pl.pallas_call(
        functools.partial(_flash_kernel, tile=tile),
        grid=(B, H, nq),
        in_specs=[pl.BlockSpec((tile, LANES), qmap), pl.BlockSpec((tile, LANES), qmap),
                  pl.BlockSpec((S, LANES), kmap),
                  pl.BlockSpec((S, LANES), lambda b, h, i: (b, 0)),
                  pl.BlockSpec((S, LANES), kmap)],
        out_specs=pl.BlockSpec((tile, LANES), qmap),
        out_shape=jax.ShapeDtypeStruct((T, H * MLA_V), MXU_DTYPE),
        compiler_params=_params("parallel", "parallel", "arbitrary"),
        name="mla_flash",
    )(qn, qr, kn, kr, v)


def _mla_layer(h, S, w_in, q_norm, kv_norm, w_q_b, w_kv_b, w_out, g, b, alpha):
    T, D = h.shape
    (qn, qr, kn, v, kr), H = _mla_project(h, w_in, q_norm, kv_norm, w_q_b, w_kv_b, S)
    o = _flash_attention(qn, qr, kn, kr, v, T // S, S, H)
    return _proj_res_ln(o, w_out.astype(MXU_DTYPE), jnp.zeros((D,), F32), h, g, b, alpha)


def _shift_delta(x, prev_rows, is_seq_start):
    prev = jnp.where(is_seq_start, 0.0, prev_rows[SUBLANES - 1:SUBLANES, :])
    shifted = pltpu.roll(x, 1, 0)
    first = lax.broadcasted_iota(jnp.int32, x.shape, 0) == 0
    return jnp.where(first, prev, shifted) - x


def _rwkv_rkv_kernel(x_ref, prev_ref, mix_ref, w_ref, o_ref, *, tiles_per_seq):
    i = pl.program_id(1)
    x = x_ref[...]
    xx = _shift_delta(x, prev_ref[...], i % tiles_per_seq == 0)
    xm = (x + xx * mix_ref[0]).astype(MXU_DTYPE)
    o_ref[0] = jnp.dot(xm, w_ref[0], preferred_element_type=F32)


def _rwkv_rkv(x, mix3, w3, S):
    T, D = x.shape
    tm = _tile(S, 512)
    per8 = tm // SUBLANES
    return pl.pallas_call(
        functools.partial(_rwkv_rkv_kernel, tiles_per_seq=S // tm),
        grid=(3, T // tm),
        in_specs=[pl.BlockSpec((tm, D), lambda n, i: (i, 0)),
                  pl.BlockSpec((SUBLANES, D), lambda n, i: (jnp.maximum(i * per8 - 1, 0), 0)),
                  pl.BlockSpec((1, 1, D), lambda n, i: (n, 0, 0)),
                  pl.BlockSpec((1, D, D), lambda n, i: (n, 0, 0))],
        out_specs=pl.BlockSpec((1, tm, D), lambda n, i: (n, i, 0)),
        out_shape=jax.ShapeDtypeStruct((3, T, D), F32),
        compiler_params=_params("parallel", "parallel"),
        name="rwkv_rkv",
    )(x, x, mix3, w3)


def _rwkv_lora_kernel(x_ref, prev_ref, mix_ref, w0_ref, a0_ref, w1_ref, w2_ref, a1_ref,
                      a2_ref, g1_ref, g2_ref, ld_ref, a_ref, g_ref, *, tiles_per_seq):
    i = pl.program_id(0)
    x = x_ref[...]
    xx = _shift_delta(x, prev_ref[...], i % tiles_per_seq == 0)
    xw = (x + xx * mix_ref[0:1, :]).astype(MXU_DTYPE)
    xa = (x + xx * mix_ref[1:2, :]).astype(MXU_DTYPE)
    xg = (x + xx * mix_ref[2:3, :]).astype(MXU_DTYPE)
    hw = jnp.tanh(jnp.dot(xw, w1_ref[...], preferred_element_type=F32)).astype(MXU_DTYPE)
    u = w0_ref[...] + jnp.dot(hw, w2_ref[...], preferred_element_type=F32)
    ld_ref[...] = -math.exp(-0.5) * _sigmoid(u)
    ha = jnp.dot(xa, a1_ref[...], preferred_element_type=F32).astype(MXU_DTYPE)
    a_ref[...] = _sigmoid(a0_ref[...] + jnp.dot(ha, a2_ref[...], preferred_element_type=F32))
    hg = _sigmoid(jnp.dot(xg, g1_ref[...], preferred_element_type=F32)).astype(MXU_DTYPE)
    g_ref[...] = jnp.dot(hg, g2_ref[...], preferred_element_type=F32)


def _pad_rank(w_down, w_up):
    r = w_down.shape[1]
    rp = -(-r // LANES) * LANES
    return (jnp.pad(w_down, ((0, 0), (0, rp - r))).astype(MXU_DTYPE),
            jnp.pad(w_up, ((0, rp - r), (0, 0))).astype(MXU_DTYPE))


def _rwkv_lora(x, mix3, w0, a0, w1, w2, a1, a2, g1, g2, S):
    T, D = x.shape
    tm = _tile(S, 512)
    per8 = tm // SUBLANES
    row = lambda i: (i, 0)
    out = jax.ShapeDtypeStruct((T, D), F32)
    w1p, w2p = _pad_rank(w1, w2)
    a1p, a2p = _pad_rank(a1, a2)
    g1p, g2p = _pad_rank(g1, g2)
    return pl.pallas_call(
        functools.partial(_rwkv_lora_kernel, tiles_per_seq=S // tm),
        grid=(T // tm,),
        in_specs=[pl.BlockSpec((tm, D), row),
                  pl.BlockSpec((SUBLANES, D), lambda i: (jnp.maximum(i * per8 - 1, 0), 0)),
                  _resident((3, D)), _resident((1, D)), _resident((1, D)),
                  _resident(w1p.shape), _resident(w2p.shape), _resident(a1p.shape),
                  _resident(a2p.shape), _resident(g1p.shape), _resident(g2p.shape)],
        out_specs=[pl.BlockSpec((tm, D), row)] * 3,
        out_shape=[out, out, out],
        compiler_params=_params("parallel"),
        name="rwkv_lora",
    )(x, x, mix3, w0.reshape(1, D), a0.reshape(1, D), w1p, w2p, a1p, a2p, g1p, g2p)


def _wkv_kernel(r_ref, k_ref, v_ref, ld_ref, a_ref, g_ref, kk_ref, ka_ref, rk_ref,
                lng_ref, lnb_ref, z_ref, state_ref, *, pairs, chunk):
    L = chunk
    c_idx = pl.program_id(2)

    @pl.when(c_idx == 0)
    def _():
        state_ref[...] = jnp.zeros_like(state_ref)

    lane = lax.broadcasted_iota(jnp.int32, (1, LANES), 1)
    head0 = lane < RWKV_HEAD

    def head_sum(t):
        s0 = jnp.sum(jnp.where(head0, t, 0.0), -1, keepdims=True)
        s1 = jnp.sum(jnp.where(head0, 0.0, t), -1, keepdims=True)
        return jnp.where(head0, s0, s1)

    def stack(t):
        return jnp.concatenate([jnp.where(head0, t, 0.0), jnp.where(head0, 0.0, t)], 0)

    ld_all = ld_ref[...]
    tri = (lax.broadcasted_iota(jnp.int32, (L, L), 0)
           >= lax.broadcasted_iota(jnp.int32, (L, L), 1)).astype(MXU_DTYPE)
    ld_hi = ld_all.astype(MXU_DTYPE)
    rem = ld_all - ld_hi.astype(F32)
    ld_mid = rem.astype(MXU_DTYPE)
    ld_lo = (rem - ld_mid.astype(F32)).astype(MXU_DTYPE)
    cum_all = (jnp.dot(tri, ld_hi, preferred_element_type=F32)
               + jnp.dot(tri, ld_mid, preferred_element_type=F32)
               + jnp.dot(tri, ld_lo, preferred_element_type=F32))

    row2 = lax.broadcasted_iota(jnp.int32, (2 * L, 2 * L), 0)
    col2 = lax.broadcasted_iota(jnp.int32, (2 * L, 2 * L), 1)
    same_head = (row2 // L) == (col2 // L)
    strict = same_head & ((row2 % L) > (col2 % L))
    incl = same_head & ((row2 % L) >= (col2 % L))
    eye = (row2 == col2).astype(F32)

    for p in range(pairs):
        sl = slice(p * LANES, (p + 1) * LANES)
        r = r_ref[:, sl]
        k = k_ref[:, sl]
        v = v_ref[:, sl]
        a = a_ref[:, sl]
        ld = ld_all[:, sl]
        cum = cum_all[:, sl]

        kk = k * kk_ref[:, sl]
        kk = kk / jnp.maximum(jnp.sqrt(head_sum(kk * kk)), 1e-12)
        k_mod = k * (1.0 + (a - 1.0) * ka_ref[:, sl])
        bb = kk * a

        cum_last = cum[L - 1:L, :]
        e_pos = jnp.exp(cum)
        e_neg = jnp.exp(-cum)
        e_prev = jnp.exp(cum - ld)
        e_tail = jnp.exp(cum_last - cum)
        w_chunk = jnp.exp(cum_last)

        lhs = jnp.concatenate([stack(-kk * e_prev), stack(r * e_pos)], 0).astype(MXU_DTYPE)
        rhs = jnp.concatenate([stack(bb * e_neg), stack(k_mod * e_neg)], 0).astype(MXU_DTYPE)
        gram = _mm_nt(lhs, rhs)
        n_ab = jnp.where(strict, gram[:2 * L, :2 * L], 0.0)
        m_ak = jnp.where(strict, gram[:2 * L, 2 * L:], 0.0)
        n_rb = jnp.where(incl, gram[2 * L:, :2 * L], 0.0)
        n_rk = jnp.where(incl, gram[2 * L:, 2 * L:], 0.0)

        t_inv = eye + n_ab
        n_pow = n_ab
        for _ in range(int(math.log2(L)) - 1):
            n_pow = _mm3(n_pow, n_pow)
            t_inv = t_inv + _mm3(t_inv, n_pow)

        state = state_ref[p]
        v2 = stack(v).astype(MXU_DTYPE)
        from_state = _mm_nt(lhs, state)
        x = from_state[:2 * L] + _mm(m_ak, v2)
        sa = _mm3(t_inv, x)
        sa_v = jnp.concatenate([sa.astype(MXU_DTYPE), v2], 0)
        y2 = from_state[2 * L:] + _mm(jnp.concatenate([n_rb, n_rk], 1), sa_v)
        y = y2[:L] + y2[L:]

        tails = jnp.concatenate([stack(bb * e_tail), stack(k_mod * e_tail)], 0)
        state_ref[p] = state * w_chunk + _mm_tn(sa_v, tails)

        mu = head_sum(y) * (1.0 / RWKV_HEAD)
        d = y - mu
        var = head_sum(d * d) * (1.0 / RWKV_HEAD)
        yn = d * lax.rsqrt(var + RWKV_GN_EPS) * lng_ref[:, sl] + lnb_ref[:, sl]
        bonus = head_sum(r * k_mod * rk_ref[:, sl]) * v
        z_ref[:, sl] = ((yn + bonus) * g_ref[:, sl]).astype(z_ref.dtype)


def _wkv(rkv, ld, a, g, k_k, k_a, r_k, ln_g, ln_b, B, S):
    _, T, D = rkv.shape
    L = RWKV_CHUNK
    n_pairs = D // LANES
    pairs = _tile(n_pairs, 4)
    width = pairs * LANES
    nc = S // L
    tok = lambda b, q, c: (b * nc + c, q)
    par = lambda b, q, c: (0, q)

    def rkv_spec(n):
        return pl.BlockSpec((None, L, width), lambda b, q, c: (n, b * nc + c, q))

    vec = lambda t: t.reshape(1, D)
    return pl.pallas_call(
        functools.partial(_wkv_kernel, pairs=pairs, chunk=L),
        grid=(B, n_pairs // pairs, nc),
        in_specs=[rkv_spec(0), rkv_spec(1), rkv_spec(2),
                  pl.BlockSpec((L, width), tok), pl.BlockSpec((L, width), tok),
                  pl.BlockSpec((L, width), tok)] + [pl.BlockSpec((1, width), par)] * 5,
        out_specs=pl.BlockSpec((L, width), tok),
        out_shape=jax.ShapeDtypeStruct((T, D), MXU_DTYPE),
        scratch_shapes=[pltpu.VMEM((pairs, LANES, LANES), F32)],
        compiler_params=_params("parallel", "parallel", "arbitrary"),
        name="wkv7",
    )(rkv, rkv, rkv, ld, a, g, vec(k_k), vec(k_a), vec(r_k), vec(ln_g), vec(ln_b))


def _rwkv_layer(h, S, mix, w_in, w0, w1, w2, a0, a1, a2, g1, g2, k_k, k_a, r_k,
                ln_g, ln_b, w_out, g, b, alpha):
    T, D = h.shape
    rkv = _rwkv_rkv(h, mix[jnp.array([0, 2, 3])].reshape(3, 1, D), w_in.astype(MXU_DTYPE), S)
    ld, a, gate = _rwkv_lora(h, mix[jnp.array([1, 4, 5])], w0, a0, w1, w2, a1, a2, g1, g2, S)
    z = _wkv(rkv, ld, a, gate, k_k, k_a, r_k, ln_g, ln_b, T // S, S)
    return _proj_res_ln(z, w_out.astype(MXU_DTYPE), jnp.zeros((D,), F32), h, g, b, alpha)


def _swa_qkv_kernel(x_ref, w_ref, b_ref, q_ref, k_ref, v_ref, *, q_width, kv_width, scale):
    y = jnp.dot(x_ref[...].astype(MXU_DTYPE), w_ref[...], preferred_element_type=F32) + b_ref[...]
    q_ref[...] = (y[:, :q_width] * scale).astype(q_ref.dtype)
    k_ref[...] = y[:, q_width:q_width + kv_width].astype(k_ref.dtype)
    v_ref[...] = y[:, q_width + kv_width:].astype(v_ref.dtype)


def _swa_qkv(x, w_in, b_in, Hq, Hk):
    T, D = x.shape
    Dh = SWA_HEAD
    qw = Hq * Dh

    def dup(t):
        t3 = t[..., qw:].reshape(t.shape[:-1] + (2, Hk, Dh))
        return jnp.concatenate([t3, t3], -1).reshape(t.shape[:-1] + (2 * Hk * 2 * Dh,))

    w = jnp.concatenate([w_in[:, :qw], dup(w_in)], -1).astype(MXU_DTYPE)
    bias = jnp.concatenate([b_in[:qw], dup(b_in)], -1).reshape(1, -1)
    kvw = Hk * 2 * Dh
    tm = _tile(T, 512)
    row = lambda i: (i, 0)
    return pl.pallas_call(
        functools.partial(_swa_qkv_kernel, q_width=qw, kv_width=kvw, scale=Dh ** -0.5),
        grid=(T // tm,),
        in_specs=[pl.BlockSpec((tm, D), row), _resident(w.shape), _resident(bias.shape)],
        out_specs=[pl.BlockSpec((tm, qw), row), pl.BlockSpec((tm, kvw), row),
                   pl.BlockSpec((tm, kvw), row)],
        out_shape=[jax.ShapeDtypeStruct((T, qw), MXU_DTYPE),
                   jax.ShapeDtypeStruct((T, kvw), MXU_DTYPE),
                   jax.ShapeDtypeStruct((T, kvw), MXU_DTYPE)],
        compiler_params=_params("parallel"),
        name="swa_qkv",
    )(x, w, bias)


def _swa_bias_kernel(bucket_ref, window_ref, rel_ref, o_ref, *, buckets):
    h = pl.program_id(0)
    bucket = bucket_ref[...]
    acc = jnp.zeros(bucket.shape, F32)
    for n in range(buckets):
        acc = jnp.where(bucket == n, rel_ref[n, h], acc)
    o_ref[0] = jnp.where(window_ref[...] > 0, acc, NEG_INF)


def _t5_bucket(dist):
    max_exact = REL_BUCKETS // 2
    n = jnp.maximum(dist, 0)
    nf = jnp.maximum(n, 1).astype(F32)
    large = max_exact + (jnp.log(nf / max_exact) / math.log(SWA_WINDOW / max_exact)
                         * (REL_BUCKETS - max_exact)).astype(jnp.int32)
    large = jnp.minimum(large, REL_BUCKETS - 1)
    return jnp.where(n < max_exact, n, large)


def _swa_bias(rel_bias):
    W = SWA_WINDOW
    Hq = rel_bias.shape[1]
    dist = jnp.arange(W)[:, None] + W - jnp.arange(2 * W)[None, :]
    bucket = _t5_bucket(dist).astype(jnp.int32)
    in_window = ((dist >= 0) & (dist < W)).astype(jnp.int32)
    return pl.pallas_call(
        functools.partial(_swa_bias_kernel, buckets=REL_BUCKETS),
        grid=(Hq,),
        in_specs=[_resident((W, 2 * W)), _resident((W, 2 * W)),
                  pl.BlockSpec(memory_space=pltpu.SMEM)],
        out_specs=pl.BlockSpec((1, W, 2 * W), lambda h: (h, 0, 0)),
        out_shape=jax.ShapeDtypeStruct((Hq, W, 2 * W), F32),
        compiler_params=_params("arbitrary"),
        name="swa_bias",
    )(bucket, in_window, rel_bias)


def _swa_attn_kernel(q_ref, kc_ref, kp_ref, vc_ref, vp_ref, bias_ref, sink_ref, o_ref,
                     *, kv_heads, group):
    W = SWA_WINDOW
    i = pl.program_id(1)
    lane = lax.broadcasted_iota(jnp.int32, (1, LANES), 1)
    half0 = lane < SWA_HEAD
    rows = group * W
    col = lax.broadcasted_iota(jnp.int32, (rows, 2 * W), 1)
    key_ok = (col >= W) | (i > 0)
    for kv in range(kv_heads):
        ksl = slice(kv * LANES, (kv + 1) * LANES)
        k_band = jnp.concatenate([kp_ref[:, ksl], kc_ref[:, ksl]], 0)
        v_band = jnp.concatenate([vp_ref[:, ksl], vc_ref[:, ksl]], 0)
        parts = []
        sinks = []
        for pair in range(group // 2):
            blk = kv * (group // 2) + pair
            qp = q_ref[:, blk * LANES:(blk + 1) * LANES]
            zero = jnp.zeros_like(qp)
            parts += [jnp.where(half0, qp, zero), jnp.where(half0, zero, qp)]
            sinks += [jnp.full((W, 1), sink_ref[2 * blk], F32),
                      jnp.full((W, 1), sink_ref[2 * blk + 1], F32)]
        q_rows = jnp.concatenate(parts, 0)
        sink = jnp.concatenate(sinks, 0)
        s = _mm_nt(q_rows, k_band)
        s = s + bias_ref[kv * group:(kv + 1) * group].reshape(rows, 2 * W)
        s = jnp.where(key_ok, s, NEG_INF)
        m = jnp.maximum(jnp.max(s, -1, keepdims=True), sink)
        p = jnp.exp(s - m)
        denom = jnp.sum(p, -1, keepdims=True) + jnp.exp(sink - m)
        o = jnp.dot(p.astype(MXU_DTYPE), v_band, preferred_element_type=F32) / denom
        for pair in range(group // 2):
            blk = kv * (group // 2) + pair
            o0 = o[(2 * pair) * W:(2 * pair + 1) * W]
            o1 = o[(2 * pair + 1) * W:(2 * pair + 2) * W]
            o_ref[:, blk * LANES:(blk + 1) * LANES] = jnp.where(half0, o0, o1).astype(o_ref.dtype)


def _swa_attention(q, kd, vd, bias, sinks, B, S, Hq, Hk):
    T = B * S
    W = SWA_WINDOW
    nb = S // W
    qw = Hq * SWA_HEAD
    kvw = Hk * LANES
    cur = lambda b, i: (b * nb + i, 0)
    prev = lambda b, i: (b * nb + jnp.maximum(i - 1, 0), 0)
    return pl.pallas_call(
        functools.partial(_swa_attn_kernel, kv_heads=Hk, group=Hq // Hk),
        grid=(B, nb),
        in_specs=[pl.BlockSpec((W, qw), cur),
                  pl.BlockSpec((W, kvw), cur), pl.BlockSpec((W, kvw), prev),
                  pl.BlockSpec((W, kvw), cur), pl.BlockSpec((W, kvw), prev),
                  _resident(bias.shape), pl.BlockSpec(memory_space=pltpu.SMEM)],
        out_specs=pl.BlockSpec((W, qw), cur),
        out_shape=jax.ShapeDtypeStruct((T, qw), MXU_DTYPE),
        compiler_params=_params("parallel", "arbitrary"),
        name="swa_attn",
    )(q, kd, kd, vd, vd, bias, sinks)


def _swa_layer(h, S, w_in, b_in, sinks, w_out, b_out, rel_bias, g, b, alpha):
    T, D = h.shape
    Hq = D // SWA_HEAD
    Hk = (w_in.shape[1] - D) // (2 * SWA_HEAD)
    assert (Hq // Hk) % 2 == 0
    q, kd, vd = _swa_qkv(h, w_in, b_in, Hq, Hk)
    bias = _swa_bias(rel_bias)
    o = _swa_attention(q, kd, vd, bias, sinks, T // S, S, Hq, Hk)
    return _proj_res_ln(o, w_out.astype(MXU_DTYPE), b_out, h, g, b, alpha)


def kernel(x, mla_w_in, mla_q_norm, mla_kv_norm, mla_w_q_b, mla_w_kv_b, mla_w_out, rwkv_mix, rwkv_w_in, rwkv_w0, rwkv_w1, rwkv_w2, rwkv_a0, rwkv_a1, rwkv_a2, rwkv_g1, rwkv_g2, rwkv_k_k, rwkv_k_a, rwkv_r_k, rwkv_ln_g, rwkv_ln_b, rwkv_w_out, swa_w_in, swa_b_in, swa_sinks, swa_w_out, swa_b_out, rel_bias, ln_g, ln_b, mlp_up, mlp_down):
    B, S, D = x.shape
    depth = ln_g.shape[0]
    alpha = (2.0 * depth) ** 0.25
    h = x.reshape(B * S, D)
    for i in range(depth):
        j = i // 3
        kind = i % 3
        g0, b0 = ln_g[i, 0], ln_b[i, 0]
        if kind == 0:
            h = _mla_layer(h, S, mla_w_in[j], mla_q_norm[j], mla_kv_norm[j], mla_w_q_b[j],
                           mla_w_kv_b[j], mla_w_out[j], g0, b0, alpha)
        elif kind == 1:
            h = _rwkv_layer(h, S, rwkv_mix[j], rwkv_w_in[j], rwkv_w0[j], rwkv_w1[j], rwkv_w2[j],
                            rwkv_a0[j], rwkv_a1[j], rwkv_a2[j], rwkv_g1[j], rwkv_g2[j],
                            rwkv_k_k[j], rwkv_k_a[j], rwkv_r_k[j].reshape(-1), rwkv_ln_g[j],
                            rwkv_ln_b[j], rwkv_w_out[j], g0, b0, alpha)
        else:
            h = _swa_layer(h, S, swa_w_in[j], swa_b_in[j], swa_sinks[j], swa_w_out[j],
                           swa_b_out[j], rel_bias, g0, b0, alpha)
        h = _mlp_res_ln(h, mlp_up[i].astype(MXU_DTYPE), mlp_down[i].astype(MXU_DTYPE),
                        ln_g[i, 1], ln_b[i, 1], alpha)
    return h.reshape(B, S, D)
```

```python
import functools
import math

import jax
import jax.numpy as jnp
from jax import lax
from jax.experimental import pallas as pl
from jax.experimental.pallas import tpu as pltpu

F32 = jnp.float32
MXU_DTYPE = jnp.bfloat16

LN_EPS = 1e-5
RMS_EPS = 1e-6
NEG_INF = -1e30
ROPE_THETA = 10000.0

LANES = 128
SUBLANES = 8
VMEM_LIMIT = 56 * 1024 * 1024

MLA_NOPE = 128
MLA_ROPE = 64
MLA_V = 128
RWKV_HEAD = 64
RWKV_GN_EPS = 64e-5
RWKV_CHUNK = 64
SWA_HEAD = 64
SWA_WINDOW = 128
REL_BUCKETS = 32


def _params(*sem):
    return pltpu.CompilerParams(dimension_semantics=sem, vmem_limit_bytes=VMEM_LIMIT)


def _tile(n, want):
    t = min(n, want)
    assert n % t == 0, (n, t)
    return t


def _resident(shape):
    nd = len(shape)
    return pl.BlockSpec(shape, lambda *_: (0,) * nd, pipeline_mode=pl.Buffered(1))


def _mm(a, b):
    return jnp.dot(a.astype(MXU_DTYPE), b.astype(MXU_DTYPE), preferred_element_type=F32)


def _mm_nt(a, b):
    return lax.dot_general(a.astype(MXU_DTYPE), b.astype(MXU_DTYPE),
                           (((1,), (1,)), ((), ())), preferred_element_type=F32)


def _mm_tn(a, b):
    return lax.dot_general(a.astype(MXU_DTYPE), b.astype(MXU_DTYPE),
                           (((0,), (0,)), ((), ())), preferred_element_type=F32)


def _split(a):
    hi = a.astype(MXU_DTYPE)
    lo = (a - hi.astype(F32)).astype(MXU_DTYPE)
    return hi, lo


def _mm3(a, b):
    ah, al = _split(a)
    bh, bl = _split(b)
    return (jnp.dot(ah, bh, preferred_element_type=F32)
            + jnp.dot(ah, bl, preferred_element_type=F32)
            + jnp.dot(al, bh, preferred_element_type=F32))


def _layer_norm(y, g, b):
    mu = jnp.mean(y, -1, keepdims=True)
    d = y - mu
    var = jnp.mean(d * d, -1, keepdims=True)
    return d * lax.rsqrt(var + LN_EPS) * g + b


def _rms_norm(y, g):
    return y * lax.rsqrt(jnp.mean(y * y, -1, keepdims=True) + RMS_EPS) * g


def _sigmoid(z):
    return 1.0 / (1.0 + jnp.exp(-z))


def _proj_res_ln_kernel(a_ref, w_ref, bias_ref, x_ref, g_ref, b_ref, o_ref, *, alpha):
    y = jnp.dot(a_ref[...], w_ref[...], preferred_element_type=F32)
    y = y + bias_ref[...] + alpha * x_ref[...]
    o_ref[...] = _layer_norm(y, g_ref[...], b_ref[...])


def _proj_res_ln(a, w, bias, x, g, b, alpha):
    T, K = a.shape
    D = w.shape[1]
    tm = _tile(T, 512)
    return pl.pallas_call(
        functools.partial(_proj_res_ln_kernel, alpha=alpha),
        grid=(T // tm,),
        in_specs=[pl.BlockSpec((tm, K), lambda i: (i, 0)),
                  _resident((K, D)), _resident((1, D)),
                  pl.BlockSpec((tm, D), lambda i: (i, 0)),
                  _resident((1, D)), _resident((1, D))],
        out_specs=pl.BlockSpec((tm, D), lambda i: (i, 0)),
        out_shape=jax.ShapeDtypeStruct((T, D), F32),
        compiler_params=_params("parallel"),
        name="proj_res_ln",
    )(a, w, bias.reshape(1, D), x, g.reshape(1, D), b.reshape(1, D))


def _mlp_kernel(x_ref, wu_ref, wd_ref, g_ref, b_ref, o_ref, xb_ref, acc_ref, *, alpha):
    j = pl.program_id(1)

    @pl.when(j == 0)
    def _():
        x = x_ref[...]
        xb_ref[...] = x.astype(MXU_DTYPE)
        acc_ref[...] = alpha * x

    u = jnp.dot(xb_ref[...], wu_ref[...], preferred_element_type=F32)
    u = jnp.maximum(u, 0.0)
    u = (u * u).astype(MXU_DTYPE)
    acc_ref[...] += jnp.dot(u, wd_ref[...], preferred_element_type=F32)

    @pl.when(j == pl.num_programs(1) - 1)
    def _():
        o_ref[...] = _layer_norm(acc_ref[...], g_ref[...], b_ref[...])


def _mlp_res_ln(x, wu, wd, g, b, alpha):
    T, D = x.shape
    Hd = wu.shape[1]
    tm = _tile(T, 512)
    th = _tile(Hd, 1024)
    return pl.pallas_call(
        functools.partial(_mlp_kernel, alpha=alpha),
        grid=(T // tm, Hd // th),
        in_specs=[pl.BlockSpec((tm, D), lambda i, j: (i, 0)),
                  pl.BlockSpec((D, th), lambda i, j: (0, j)),
                  pl.BlockSpec((th, D), lambda i, j: (j, 0)),
                  _resident((1, D)), _resident((1, D))],
        out_specs=pl.BlockSpec((tm, D), lambda i, j: (i, 0)),
        out_shape=jax.ShapeDtypeStruct((T, D), F32),
        scratch_shapes=[pltpu.VMEM((tm, D), MXU_DTYPE), pltpu.VMEM((tm, D), F32)],
        compiler_params=_params("parallel", "arbitrary"),
        name="mlp_res_ln",
    )(x, wu, wd, g.reshape(1, D), b.reshape(1, D))


def _mla_proj_kernel(x_ref, wqa_ref, wkva_ref, wkr_ref, wkrr_ref, qg_ref, kvg_ref,
                     wqn_ref, wqr_ref, wqrr_ref, wkn_ref, wv_ref, cos_ref, sin_ref,
                     qn_ref, qr_ref, kn_ref, v_ref, kr_ref, *, heads, scale):
    xb = x_ref[...].astype(MXU_DTYPE)
    cos = cos_ref[...]
    sin = sin_ref[...]
    q_lat = jnp.dot(xb, wqa_ref[...], preferred_element_type=F32)
    kv_lat = jnp.dot(xb, wkva_ref[...], preferred_element_type=F32)
    kr = jnp.dot(xb, wkr_ref[...], preferred_element_type=F32)
    krr = jnp.dot(xb, wkrr_ref[...], preferred_element_type=F32)
    kr_ref[...] = (kr * cos + krr * sin).astype(kr_ref.dtype)

    qn = _rms_norm(q_lat, qg_ref[...]).astype(MXU_DTYPE)
    qn_ref[...] = (jnp.dot(qn, wqn_ref[...], preferred_element_type=F32) * scale).astype(qn_ref.dtype)
    qr = jnp.dot(qn, wqr_ref[...], preferred_element_type=F32)
    qrr = jnp.dot(qn, wqrr_ref[...], preferred_element_type=F32)
    for h in range(heads):
        sl = slice(h * LANES, (h + 1) * LANES)
        qr_ref[:, sl] = ((qr[:, sl] * cos + qrr[:, sl] * sin) * scale).astype(qr_ref.dtype)

    kvn = _rms_norm(kv_lat, kvg_ref[...]).astype(MXU_DTYPE)
    kn_ref[...] = jnp.dot(kvn, wkn_ref[...], preferred_element_type=F32).astype(kn_ref.dtype)
    v_ref[...] = jnp.dot(kvn, wv_ref[...], preferred_element_type=F32).astype(v_ref.dtype)


def _rot_half_cols(w):
    half = w.shape[-1] // 2
    return jnp.concatenate([-w[..., half:], w[..., :half]], -1)


def _pad_lanes(w, width):
    pad = [(0, 0)] * (w.ndim - 1) + [(0, width - w.shape[-1])]
    return jnp.pad(w, pad)


def _mla_project(x, w_in, q_norm, kv_norm, w_q_b, w_kv_b, S):
    T, D = x.shape
    q_rank = q_norm.shape[-1]
    kv_rank = kv_norm.shape[-1]
    H = w_q_b.shape[1] // (MLA_NOPE + MLA_ROPE)
    scale = (MLA_NOPE + MLA_ROPE) ** -0.5 * math.log2(math.e)

    wqa = w_in[:, :q_rank].astype(MXU_DTYPE)
    wkva = w_in[:, q_rank:q_rank + kv_rank].astype(MXU_DTYPE)
    wkr_raw = w_in[:, q_rank + kv_rank:]
    wkr = _pad_lanes(wkr_raw, LANES).astype(MXU_DTYPE)
    wkrr = _pad_lanes(_rot_half_cols(wkr_raw), LANES).astype(MXU_DTYPE)
    wq3 = w_q_b.reshape(q_rank, H, MLA_NOPE + MLA_ROPE)
    wqn = wq3[:, :, :MLA_NOPE].reshape(q_rank, H * MLA_NOPE).astype(MXU_DTYPE)
    wq_rope = wq3[:, :, MLA_NOPE:]
    wqr = _pad_lanes(wq_rope, LANES).reshape(q_rank, H * LANES).astype(MXU_DTYPE)
    wqrr = _pad_lanes(_rot_half_cols(wq_rope), LANES).reshape(q_rank, H * LANES).astype(MXU_DTYPE)
    wkv3 = w_kv_b.reshape(kv_rank, H, MLA_NOPE + MLA_V)
    wkn = wkv3[:, :, :MLA_NOPE].reshape(kv_rank, H * MLA_NOPE).astype(MXU_DTYPE)
    wv = wkv3[:, :, MLA_NOPE:].reshape(kv_rank, H * MLA_V).astype(MXU_DTYPE)

    half = MLA_ROPE // 2
    inv = ROPE_THETA ** (-jnp.arange(half, dtype=F32) / half)
    ang = jnp.arange(S, dtype=F32)[:, None] * inv[None, :]
    cos = _pad_lanes(jnp.concatenate([jnp.cos(ang), jnp.cos(ang)], -1), LANES)
    sin = _pad_lanes(jnp.concatenate([jnp.sin(ang), jnp.sin(ang)], -1), LANES)

    tm = _tile(S, 512)
    ns = S // tm
    row = lambda i: (i, 0)
    pos = lambda i: (i % ns, 0)
    wide = jax.ShapeDtypeStruct((T, H * LANES), MXU_DTYPE)
    outs = pl.pallas_call(
        functools.partial(_mla_proj_kernel, heads=H, scale=scale),
        grid=(T // tm,),
        in_specs=[pl.BlockSpec((tm, D), row),
                  _resident(wqa.shape), _resident(wkva.shape), _resident(wkr.shape),
                  _resident(wkrr.shape), _resident((1, q_rank)), _resident((1, kv_rank)),
                  _resident(wqn.shape), _resident(wqr.shape), _resident(wqrr.shape),
                  _resident(wkn.shape), _resident(wv.shape),
                  pl.BlockSpec((tm, LANES), pos), pl.BlockSpec((tm, LANES), pos)],
        out_specs=[pl.BlockSpec((tm, H * LANES), row)] * 4 + [pl.BlockSpec((tm, LANES), row)],
        out_shape=[wide, wide, wide, wide, jax.ShapeDtypeStruct((T, LANES), MXU_DTYPE)],
        compiler_params=_params("parallel"),
        name="mla_proj",
    )(x, wqa, wkva, wkr, wkrr, q_norm.reshape(1, -1), kv_norm.reshape(1, -1),
      wqn, wqr, wqrr, wkn, wv, cos, sin)
    return outs, H


def _flash_kernel(qn_ref, qr_ref, kn_ref, kr_ref, v_ref, o_ref, *, tile):
    i = pl.program_id(2)
    q = jnp.concatenate([qn_ref[...], qr_ref[...]], -1)

    def tile_rows(j):
        return pl.ds(pl.multiple_of(j * tile, tile), tile)

    def scores(j):
        rows = tile_rows(j)
        return _mm_nt(q, jnp.concatenate([kn_ref[rows, :], kr_ref[rows, :]], -1))

    def update(s, j, m, l, acc):
        m_new = jnp.maximum(m, jnp.max(s, -1, keepdims=True))
        p = jnp.exp2(s - m_new)
        corr = jnp.exp2(m - m_new)
        l = corr * l + jnp.sum(p, -1, keepdims=True)
        acc = corr * acc + jnp.dot(p.astype(MXU_DTYPE), v_ref[tile_rows(j), :],
                                   preferred_element_type=F32)
        return m_new, l, acc

    def body(j, carry):
        return update(scores(j), j, *carry)

    init = (jnp.full((tile, 1), NEG_INF, F32), jnp.zeros((tile, 1), F32),
            jnp.zeros((tile, LANES), F32))
    m, l, acc = lax.fori_loop(0, i, body, init)
    s = scores(i)
    r = lax.broadcasted_iota(jnp.int32, s.shape, 0)
    c = lax.broadcasted_iota(jnp.int32, s.shape, 1)
    _, l, acc = update(jnp.where(c <= r, s, NEG_INF), i, m, l, acc)
    o_ref[...] = (acc / l).astype(o_ref.dtype)


def _flash_attention(qn, qr, kn, kr, v, B, S, H):
    T = B * S
    tile = _tile(S, 512)
    nq = S // tile
    qmap = lambda b, h, i: (b * nq + i, h)
    kmap = lambda b, h, i: (b, h)
    return pl.pallas_call(
        functools.partial(_flash_kernel, tile=tile),
        grid=(B, H, nq),
        in_specs=[pl.BlockSpec((tile, LANES), qmap), pl.BlockSpec((tile, LANES), qmap),
                  pl.BlockSpec((S, LANES), kmap),
                  pl.BlockSpec((S, LANES), lambda b, h, i: (b, 0)),
                  pl.BlockSpec((S, LANES), kmap)],
        out_specs=pl.BlockSpec((tile, LANES), qmap),
        out_shape=jax.ShapeDtypeStruct((T, H * MLA_V), MXU_DTYPE),
        compiler_params=_params("parallel", "parallel", "arbitrary"),
        name="mla_flash",
    )(qn, qr, kn, kr, v)


def _mla_layer(h, S, w_in, q_norm, kv_norm, w_q_b, w_kv_b, w_out, g, b, alpha):
    T, D = h.shape
    (qn, qr, kn, v, kr), H = _mla_project(h, w_in, q_norm, kv_norm, w_q_b, w_kv_b, S)
    o = _flash_attention(qn, qr, kn, kr, v, T // S, S, H)
    return _proj_res_ln(o, w_out.astype(MXU_DTYPE), jnp.zeros((D,), F32), h, g, b, alpha)


def _shift_delta(x, prev_rows, is_seq_start):
    prev = jnp.where(is_seq_start, 0.0, prev_rows[SUBLANES - 1:SUBLANES, :])
    shifted = pltpu.roll(x, 1, 0)
    first = lax.broadcasted_iota(jnp.int32, x.shape, 0) == 0
    return jnp.where(first, prev, shifted) - x


def _rwkv_rkv_kernel(x_ref, prev_ref, mix_ref, w_ref, o_ref, *, tiles_per_seq):
    i = pl.program_id(1)
    x = x_ref[...]
    xx = _shift_delta(x, prev_ref[...], i % tiles_per_seq == 0)
    xm = (x + xx * mix_ref[0]).astype(MXU_DTYPE)
    o_ref[0] = jnp.dot(xm, w_ref[0], preferred_element_type=F32)


def _rwkv_rkv(x, mix3, w3, S):
    T, D = x.shape
    tm = _tile(S, 512)
    per8 = tm // SUBLANES
    return pl.pallas_call(
        functools.partial(_rwkv_rkv_kernel, tiles_per_seq=S // tm),
        grid=(3, T // tm),
        in_specs=[pl.BlockSpec((tm, D), lambda n, i: (i, 0)),
                  pl.BlockSpec((SUBLANES, D), lambda n, i: (jnp.maximum(i * per8 - 1, 0), 0)),
                  pl.BlockSpec((1, 1, D), lambda n, i: (n, 0, 0)),
                  pl.BlockSpec((1, D, D), lambda n, i: (n, 0, 0))],
        out_specs=pl.BlockSpec((1, tm, D), lambda n, i: (n, i, 0)),
        out_shape=jax.ShapeDtypeStruct((3, T, D), F32),
        compiler_params=_params("parallel", "parallel"),
        name="rwkv_rkv",
    )(x, x, mix3, w3)


def _rwkv_lora_kernel(x_ref, prev_ref, mix_ref, w0_ref, a0_ref, w1_ref, w2_ref, a1_ref,
                      a2_ref, g1_ref, g2_ref, ld_ref, a_ref, g_ref, *, tiles_per_seq):
    i = pl.program_id(0)
    x = x_ref[...]
    xx = _shift_delta(x, prev_ref[...], i % tiles_per_seq == 0)
    xw = (x + xx * mix_ref[0:1, :]).astype(MXU_DTYPE)
    xa = (x + xx * mix_ref[1:2, :]).astype(MXU_DTYPE)
    xg = (x + xx * mix_ref[2:3, :]).astype(MXU_DTYPE)
    hw = jnp.tanh(jnp.dot(xw, w1_ref[...], preferred_element_type=F32)).astype(MXU_DTYPE)
    u = w0_ref[...] + jnp.dot(hw, w2_ref[...], preferred_element_type=F32)
    ld_ref[...] = -math.exp(-0.5) * _sigmoid(u)
    ha = jnp.dot(xa, a1_ref[...], preferred_element_type=F32).astype(MXU_DTYPE)
    a_ref[...] = _sigmoid(a0_ref[...] + jnp.dot(ha, a2_ref[...], preferred_element_type=F32))
    hg = _sigmoid(jnp.dot(xg, g1_ref[...], preferred_element_type=F32)).astype(MXU_DTYPE)
    g_ref[...] = jnp.dot(hg, g2_ref[...], preferred_element_type=F32)


def _pad_rank(w_down, w_up):
    r = w_down.shape[1]
    rp = -(-r // LANES) * LANES
    return (jnp.pad(w_down, ((0, 0), (0, rp - r))).astype(MXU_DTYPE),
            jnp.pad(w_up, ((0, rp - r), (0, 0))).astype(MXU_DTYPE))


def _rwkv_lora(x, mix3, w0, a0, w1, w2, a1, a2, g1, g2, S):
    T, D = x.shape
    tm = _tile(S, 512)
    per8 = tm // SUBLANES
    row = lambda i: (i, 0)
    out = jax.ShapeDtypeStruct((T, D), F32)
    w1p, w2p = _pad_rank(w1, w2)
    a1p, a2p = _pad_rank(a1, a2)
    g1p, g2p = _pad_rank(g1, g2)
    return pl.pallas_call(
        functools.partial(_rwkv_lora_kernel, tiles_per_seq=S // tm),
        grid=(T // tm,),
        in_specs=[pl.BlockSpec((tm, D), row),
                  pl.BlockSpec((SUBLANES, D), lambda i: (jnp.maximum(i * per8 - 1, 0), 0)),
                  _resident((3, D)), _resident((1, D)), _resident((1, D)),
                  _resident(w1p.shape), _resident(w2p.shape), _resident(a1p.shape),
                  _resident(a2p.shape), _resident(g1p.shape), _resident(g2p.shape)],
        out_specs=[pl.BlockSpec((tm, D), row)] * 3,
        out_shape=[out, out, out],
        compiler_params=_params("parallel"),
        name="rwkv_lora",
    )(x, x, mix3, w0.reshape(1, D), a0.reshape(1, D), w1p, w2p, a1p, a2p, g1p, g2p)


def _wkv_kernel(r_ref, k_ref, v_ref, ld_ref, a_ref, g_ref, kk_ref, ka_ref, rk_ref,
                lng_ref, lnb_ref, z_ref, state_ref, *, pairs, chunk):
    L = chunk
    c_idx = pl.program_id(2)

    @pl.when(c_idx == 0)
    def _():
        state_ref[...] = jnp.zeros_like(state_ref)

    lane = lax.broadcasted_iota(jnp.int32, (1, LANES), 1)
    head0 = lane < RWKV_HEAD

    def head_sum(t):
        s0 = jnp.sum(jnp.where(head0, t, 0.0), -1, keepdims=True)
        s1 = jnp.sum(jnp.where(head0, 0.0, t), -1, keepdims=True)
        return jnp.where(head0, s0, s1)

    def stack(t):
        return jnp.concatenate([jnp.where(head0, t, 0.0), jnp.where(head0, 0.0, t)], 0)

    ld_all = ld_ref[...]
    tri = (lax.broadcasted_iota(jnp.int32, (L, L), 0)
           >= lax.broadcasted_iota(jnp.int32, (L, L), 1)).astype(MXU_DTYPE)
    ld_hi = ld_all.astype(MXU_DTYPE)
    rem = ld_all - ld_hi.astype(F32)
    ld_mid = rem.astype(MXU_DTYPE)
    ld_lo = (rem - ld_mid.astype(F32)).astype(MXU_DTYPE)
    cum_all = (jnp.dot(tri, ld_hi, preferred_element_type=F32)
               + jnp.dot(tri, ld_mid, preferred_element_type=F32)
               + jnp.dot(tri, ld_lo, preferred_element_type=F32))

    row2 = lax.broadcasted_iota(jnp.int32, (2 * L, 2 * L), 0)
    col2 = lax.broadcasted_iota(jnp.int32, (2 * L, 2 * L), 1)
    same_head = (row2 // L) == (col2 // L)
    strict = same_head & ((row2 % L) > (col2 % L))
    incl = same_head & ((row2 % L) >= (col2 % L))
    eye = (row2 == col2).astype(F32)

    slices = [slice(p * LANES, (p + 1) * LANES) for p in range(pairs)]
    lhs, rhs, v2, tails, w_chunk = [], [], [], [], []
    for sl in slices:
        r = r_ref[:, sl]
        k = k_ref[:, sl]
        a = a_ref[:, sl]
        ld = ld_all[:, sl]
        cum = cum_all[:, sl]
        kk = k * kk_ref[:, sl]
        kk = kk / jnp.maximum(jnp.sqrt(head_sum(kk * kk)), 1e-12)
        k_mod = k * (1.0 + (a - 1.0) * ka_ref[:, sl])
        bb = kk * a
        cum_last = cum[L - 1:L, :]
        e_neg = jnp.exp(-cum)
        e_tail = jnp.exp(cum_last - cum)
        lhs.append(jnp.concatenate([stack(-kk * jnp.exp(cum - ld)), stack(r * jnp.exp(cum))],
                                   0).astype(MXU_DTYPE))
        rhs.append(jnp.concatenate([stack(bb * e_neg), stack(k_mod * e_neg)], 0).astype(MXU_DTYPE))
        tails.append(jnp.concatenate([stack(bb * e_tail), stack(k_mod * e_tail)],
                                     0).astype(MXU_DTYPE))
        v2.append(stack(v_ref[:, sl]).astype(MXU_DTYPE))
        w_chunk.append(jnp.exp(cum_last))

    gram = [_mm_nt(lhs[p], rhs[p]) for p in range(pairs)]
    n_ab = [jnp.where(strict, g_[:2 * L, :2 * L], 0.0) for g_ in gram]
    m_ak = [jnp.where(strict, g_[:2 * L, 2 * L:], 0.0).astype(MXU_DTYPE) for g_ in gram]
    n_r = [jnp.concatenate([jnp.where(incl, g_[2 * L:, :2 * L], 0.0),
                            jnp.where(incl, g_[2 * L:, 2 * L:], 0.0)], 1).astype(MXU_DTYPE)
           for g_ in gram]

    levels = int(math.log2(L))
    t_inv = [eye + n for n in n_ab]
    n_pow = [_mm(n, n) for n in n_ab]
    for _ in range(levels - 2):
        both = [_mm(jnp.concatenate([n_pow[p].astype(MXU_DTYPE), t_inv[p].astype(MXU_DTYPE)], 0),
                    n_pow[p]) for p in range(pairs)]
        n_pow = [b_[:2 * L] for b_ in both]
        t_inv = [t_inv[p] + both[p][2 * L:] for p in range(pairs)]
    t_inv = [t_inv[p] + _mm(t_inv[p], n_pow[p]) for p in range(pairs)]

    state = [state_ref[p] for p in range(pairs)]
    from_state = [_mm_nt(lhs[p], state[p]) for p in range(pairs)]
    x = [from_state[p][:2 * L] + _mm(m_ak[p], v2[p]) for p in range(pairs)]
    sa_v = [jnp.concatenate([_mm(t_inv[p], x[p]).astype(MXU_DTYPE), v2[p]], 0)
            for p in range(pairs)]
    y2 = [from_state[p][2 * L:] + _mm(n_r[p], sa_v[p]) for p in range(pairs)]
    for p in range(pairs):
        state_ref[p] = state[p] * w_chunk[p] + _mm_tn(sa_v[p], tails[p])

    for p, sl in enumerate(slices):
        y = y2[p][:L] + y2[p][L:]
        r = r_ref[:, sl]
        k_mod = k_ref[:, sl] * (1.0 + (a_ref[:, sl] - 1.0) * ka_ref[:, sl])
        mu = head_sum(y) * (1.0 / RWKV_HEAD)
        d = y - mu
        var = head_sum(d * d) * (1.0 / RWKV_HEAD)
        yn = d * lax.rsqrt(var + RWKV_GN_EPS) * lng_ref[:, sl] + lnb_ref[:, sl]
        bonus = head_sum(r * k_mod * rk_ref[:, sl]) * v_ref[:, sl]
        z_ref[:, sl] = ((yn + bonus) * g_ref[:, sl]).astype(z_ref.dtype)


def _wkv(rkv, ld, a, g, k_k, k_a, r_k, ln_g, ln_b, B, S):
    _, T, D = rkv.shape
    L = RWKV_CHUNK
    n_pairs = D // LANES
    pairs = _tile(n_pairs, 16)
    width = pairs * LANES
    nc = S // L
    tok = lambda b, q, c: (b * nc + c, q)
    par = lambda b, q, c: (0, q)

    def rkv_spec(n):
        return pl.BlockSpec((None, L, width), lambda b, q, c: (n, b * nc + c, q))

    vec = lambda t: t.reshape(1, D)
    return pl.pallas_call(
        functools.partial(_wkv_kernel, pairs=pairs, chunk=L),
        grid=(B, n_pairs // pairs, nc),
        in_specs=[rkv_spec(0), rkv_spec(1), rkv_spec(2),
                  pl.BlockSpec((L, width), tok), pl.BlockSpec((L, width), tok),
                  pl.BlockSpec((L, width), tok)] + [pl.BlockSpec((1, width), par)] * 5,
        out_specs=pl.BlockSpec((L, width), tok),
        out_shape=jax.ShapeDtypeStruct((T, D), MXU_DTYPE),
        scratch_shapes=[pltpu.VMEM((pairs, LANES, LANES), F32)],
        compiler_params=_params("parallel", "parallel", "arbitrary"),
        name="wkv7",
    )(rkv, rkv, rkv, ld, a, g, vec(k_k), vec(k_a), vec(r_k), vec(ln_g), vec(ln_b))


def _rwkv_layer(h, S, mix, w_in, w0, w1, w2, a0, a1, a2, g1, g2, k_k, k_a, r_k,
                ln_g, ln_b, w_out, g, b, alpha):
    T, D = h.shape
    rkv = _rwkv_rkv(h, mix[jnp.array([0, 2, 3])].reshape(3, 1, D), w_in.astype(MXU_DTYPE), S)
    ld, a, gate = _rwkv_lora(h, mix[jnp.array([1, 4, 5])], w0, a0, w1, w2, a1, a2, g1, g2, S)
    z = _wkv(rkv, ld, a, gate, k_k, k_a, r_k, ln_g, ln_b, T // S, S)
    return _proj_res_ln(z, w_out.astype(MXU_DTYPE), jnp.zeros((D,), F32), h, g, b, alpha)


def _swa_qkv_kernel(x_ref, w_ref, b_ref, q_ref, k_ref, v_ref, *, q_width, kv_width, scale):
    y = jnp.dot(x_ref[...].astype(MXU_DTYPE), w_ref[...], preferred_element_type=F32) + b_ref[...]
    q_ref[...] = (y[:, :q_width] * scale).astype(q_ref.dtype)
    k_ref[...] = y[:, q_width:q_width + kv_width].astype(k_ref.dtype)
    v_ref[...] = y[:, q_width + kv_width:].astype(v_ref.dtype)


def _swa_qkv(x, w_in, b_in, Hq, Hk):
    T, D = x.shape
    Dh = SWA_HEAD
    qw = Hq * Dh

    def dup(t):
        t3 = t[..., qw:].reshape(t.shape[:-1] + (2, Hk, Dh))
        return jnp.concatenate([t3, t3], -1).reshape(t.shape[:-1] + (2 * Hk * 2 * Dh,))

    w = jnp.concatenate([w_in[:, :qw], dup(w_in)], -1).astype(MXU_DTYPE)
    bias = jnp.concatenate([b_in[:qw], dup(b_in)], -1).reshape(1, -1)
    kvw = Hk * 2 * Dh
    tm = _tile(T, 512)
    row = lambda i: (i, 0)
    return pl.pallas_call(
        functools.partial(_swa_qkv_kernel, q_width=qw, kv_width=kvw, scale=Dh ** -0.5),
        grid=(T // tm,),
        in_specs=[pl.BlockSpec((tm, D), row), _resident(w.shape), _resident(bias.shape)],
        out_specs=[pl.BlockSpec((tm, qw), row), pl.BlockSpec((tm, kvw), row),
                   pl.BlockSpec((tm, kvw), row)],
        out_shape=[jax.ShapeDtypeStruct((T, qw), MXU_DTYPE),
                   jax.ShapeDtypeStruct((T, kvw), MXU_DTYPE),
                   jax.ShapeDtypeStruct((T, kvw), MXU_DTYPE)],
        compiler_params=_params("parallel"),
        name="swa_qkv",
    )(x, w, bias)


def _swa_bias_kernel(bucket_ref, window_ref, rel_ref, o_ref, *, buckets):
    h = pl.program_id(0)
    bucket = bucket_ref[...]
    acc = jnp.zeros(bucket.shape, F32)
    for n in range(buckets):
        acc = jnp.where(bucket == n, rel_ref[n, h], acc)
    o_ref[0] = jnp.where(window_ref[...] > 0, acc, NEG_INF)


def _t5_bucket(dist):
    max_exact = REL_BUCKETS // 2
    n = jnp.maximum(dist, 0)
    nf = jnp.maximum(n, 1).astype(F32)
    large = max_exact + (jnp.log(nf / max_exact) / math.log(SWA_WINDOW / max_exact)
                         * (REL_BUCKETS - max_exact)).astype(jnp.int32)
    large = jnp.minimum(large, REL_BUCKETS - 1)
    return jnp.where(n < max_exact, n, large)


def _swa_bias(rel_bias):
    W = SWA_WINDOW
    Hq = rel_bias.shape[1]
    dist = jnp.arange(W)[:, None] + W - jnp.arange(2 * W)[None, :]
    bucket = _t5_bucket(dist).astype(jnp.int32)
    in_window = ((dist >= 0) & (dist < W)).astype(jnp.int32)
    return pl.pallas_call(
        functools.partial(_swa_bias_kernel, buckets=REL_BUCKETS),
        grid=(Hq,),
        in_specs=[_resident((W, 2 * W)), _resident((W, 2 * W)),
                  pl.BlockSpec(memory_space=pltpu.SMEM)],
        out_specs=pl.BlockSpec((1, W, 2 * W), lambda h: (h, 0, 0)),
        out_shape=jax.ShapeDtypeStruct((Hq, W, 2 * W), F32),
        compiler_params=_params("arbitrary"),
        name="swa_bias",
    )(bucket, in_window, rel_bias)


def _swa_attn_kernel(q_ref, kc_ref, kp_ref, vc_ref, vp_ref, bias_ref, sink_ref, o_ref,
                     *, kv_heads, group):
    W = SWA_WINDOW
    i = pl.program_id(1)
    lane = lax.broadcasted_iota(jnp.int32, (1, LANES), 1)
    half0 = lane < SWA_HEAD
    rows = group * W
    col = lax.broadcasted_iota(jnp.int32, (rows, 2 * W), 1)
    key_ok = (col >= W) | (i > 0)
    for kv in range(kv_heads):
        ksl = slice(kv * LANES, (kv + 1) * LANES)
        k_band = jnp.concatenate([kp_ref[:, ksl], kc_ref[:, ksl]], 0)
        v_band = jnp.concatenate([vp_ref[:, ksl], vc_ref[:, ksl]], 0)
        parts = []
        sinks = []
        for pair in range(group // 2):
            blk = kv * (group // 2) + pair
            qp = q_ref[:, blk * LANES:(blk + 1) * LANES]
            zero = jnp.zeros_like(qp)
            parts += [jnp.where(half0, qp, zero), jnp.where(half0, zero, qp)]
            sinks += [jnp.full((W, 1), sink_ref[2 * blk], F32),
                      jnp.full((W, 1), sink_ref[2 * blk + 1], F32)]
        q_rows = jnp.concatenate(parts, 0)
        sink = jnp.concatenate(sinks, 0)
        s = _mm_nt(q_rows, k_band)
        s = s + bias_ref[kv * group:(kv + 1) * group].reshape(rows, 2 * W)
        s = jnp.where(key_ok, s, NEG_INF)
        m = jnp.maximum(jnp.max(s, -1, keepdims=True), sink)
        p = jnp.exp(s - m)
        denom = jnp.sum(p, -1, keepdims=True) + jnp.exp(sink - m)
        o = jnp.dot(p.astype(MXU_DTYPE), v_band, preferred_element_type=F32) / denom
        for pair in range(group // 2):
            blk = kv * (group // 2) + pair
            o0 = o[(2 * pair) * W:(2 * pair + 1) * W]
            o1 = o[(2 * pair + 1) * W:(2 * pair + 2) * W]
            o_ref[:, blk * LANES:(blk + 1) * LANES] = jnp.where(half0, o0, o1).astype(o_ref.dtype)


def _swa_attention(q, kd, vd, bias, sinks, B, S, Hq, Hk):
    T = B * S
    W = SWA_WINDOW
    nb = S // W
    qw = Hq * SWA_HEAD
    kvw = Hk * LANES
    cur = lambda b, i: (b * nb + i, 0)
    prev = lambda b, i: (b * nb + jnp.maximum(i - 1, 0), 0)
    return pl.pallas_call(
        functools.partial(_swa_attn_kernel, kv_heads=Hk, group=Hq // Hk),
        grid=(B, nb),
        in_specs=[pl.BlockSpec((W, qw), cur),
                  pl.BlockSpec((W, kvw), cur), pl.BlockSpec((W, kvw), prev),
                  pl.BlockSpec((W, kvw), cur), pl.BlockSpec((W, kvw), prev),
                  _resident(bias.shape), pl.BlockSpec(memory_space=pltpu.SMEM)],
        out_specs=pl.BlockSpec((W, qw), cur),
        out_shape=jax.ShapeDtypeStruct((T, qw), MXU_DTYPE),
        compiler_params=_params("parallel", "arbitrary"),
        name="swa_attn",
    )(q, kd, kd, vd, vd, bias, sinks)


def _swa_layer(h, S, w_in, b_in, sinks, w_out, b_out, rel_bias, g, b, alpha):
    T, D = h.shape
    Hq = D // SWA_HEAD
    Hk = (w_in.shape[1] - D) // (2 * SWA_HEAD)
    assert (Hq // Hk) % 2 == 0
    q, kd, vd = _swa_qkv(h, w_in, b_in, Hq, Hk)
    bias = _swa_bias(rel_bias)
    o = _swa_attention(q, kd, vd, bias, sinks, T // S, S, Hq, Hk)
    return _proj_res_ln(o, w_out.astype(MXU_DTYPE), b_out, h, g, b, alpha)


def kernel(x, mla_w_in, mla_q_norm, mla_kv_norm, mla_w_q_b, mla_w_kv_b, mla_w_out, rwkv_mix, rwkv_w_in, rwkv_w0, rwkv_w1, rwkv_w2, rwkv_a0, rwkv_a1, rwkv_a2, rwkv_g1, rwkv_g2, rwkv_k_k, rwkv_k_a, rwkv_r_k, rwkv_ln_g, rwkv_ln_b, rwkv_w_out, swa_w_in, swa_b_in, swa_sinks, swa_w_out, swa_b_out, rel_bias, ln_g, ln_b, mlp_up, mlp_down):
    B, S, D = x.shape
    depth = ln_g.shape[0]
    alpha = (2.0 * depth) ** 0.25
    h = x.reshape(B * S, D)
    for i in range(depth):
        j = i // 3
        kind = i % 3
        g0, b0 = ln_g[i, 0], ln_b[i, 0]
        if kind == 0:
            h = _mla_layer(h, S, mla_w_in[j], mla_q_norm[j], mla_kv_norm[j], mla_w_q_b[j],
                           mla_w_kv_b[j], mla_w_out[j], g0, b0, alpha)
        elif kind == 1:
            h = _rwkv_layer(h, S, rwkv_mix[j], rwkv_w_in[j], rwkv_w0[j], rwkv_w1[j], rwkv_w2[j],
                            rwkv_a0[j], rwkv_a1[j], rwkv_a2[j], rwkv_g1[j], rwkv_g2[j],
                            rwkv_k_k[j], rwkv_k_a[j], rwkv_r_k[j].reshape(-1), rwkv_ln_g[j],
                            rwkv_ln_b[j], rwkv_w_out[j], g0, b0, alpha)
        else:
            h = _swa_layer(h, S, swa_w_in[j], swa_b_in[j], swa_sinks[j], swa_w_out[j],
                           swa_b_out[j], rel_bias, g0, b0, alpha)
        h = _mlp_res_ln(h, mlp_up[i].astype(MXU_DTYPE), mlp_down[i].astype(MXU_DTYPE),
                        ln_g[i, 1], ln_b[i, 1], alpha)
    return h.reshape(B, S, D)
```

```python
import functools
import math

import jax
import jax.numpy as jnp
from jax import lax
from jax.experimental import pallas as pl
from jax.experimental.pallas import tpu as pltpu

F32 = jnp.float32
MXU_DTYPE = jnp.bfloat16

LN_EPS = 1e-5
RMS_EPS = 1e-6
NEG_INF = -1e30
ROPE_THETA = 10000.0

LANES = 128
SUBLANES = 8
VMEM_LIMIT = 56 * 1024 * 1024

MLA_NOPE = 128
MLA_ROPE = 64
MLA_V = 128
RWKV_HEAD = 64
RWKV_GN_EPS = 64e-5
RWKV_CHUNK = 64
SWA_HEAD = 64
SWA_WINDOW = 128
REL_BUCKETS = 32


def _params(*sem):
    return pltpu.CompilerParams(dimension_semantics=sem, vmem_limit_bytes=VMEM_LIMIT)


def _tile(n, want):
    t = min(n, want)
    assert n % t == 0, (n, t)
    return t


def _resident(shape):
    nd = len(shape)
    return pl.BlockSpec(shape, lambda *_: (0,) * nd, pipeline_mode=pl.Buffered(1))


def _mm(a, b):
    return jnp.dot(a.astype(MXU_DTYPE), b.astype(MXU_DTYPE), preferred_element_type=F32)


def _mm_nt(a, b):
    return lax.dot_general(a.astype(MXU_DTYPE), b.astype(MXU_DTYPE),
                           (((1,), (1,)), ((), ())), preferred_element_type=F32)


def _mm_tn(a, b):
    return lax.dot_general(a.astype(MXU_DTYPE), b.astype(MXU_DTYPE),
                           (((0,), (0,)), ((), ())), preferred_element_type=F32)


def _split(a):
    hi = a.astype(MXU_DTYPE)
    lo = (a - hi.astype(F32)).astype(MXU_DTYPE)
    return hi, lo


def _mm3(a, b):
    ah, al = _split(a)
    bh, bl = _split(b)
    return (jnp.dot(ah, bh, preferred_element_type=F32)
            + jnp.dot(ah, bl, preferred_element_type=F32)
            + jnp.dot(al, bh, preferred_element_type=F32))


def _layer_norm(y, g, b):
    mu = jnp.mean(y, -1, keepdims=True)
    d = y - mu
    var = jnp.mean(d * d, -1, keepdims=True)
    return d * lax.rsqrt(var + LN_EPS) * g + b


def _rms_norm(y, g):
    return y * lax.rsqrt(jnp.mean(y * y, -1, keepdims=True) + RMS_EPS) * g


def _sigmoid(z):
    return 1.0 / (1.0 + jnp.exp(-z))


def _proj_res_ln_kernel(a_ref, w_ref, bias_ref, x_ref, g_ref, b_ref, o_ref, *, alpha):
    y = jnp.dot(a_ref[...], w_ref[...], preferred_element_type=F32)
    y = y + bias_ref[...] + alpha * x_ref[...]
    o_ref[...] = _layer_norm(y, g_ref[...], b_ref[...])


def _proj_res_ln(a, w, bias, x, g, b, alpha):
    T, K = a.shape
    D = w.shape[1]
    tm = _tile(T, 512)
    return pl.pallas_call(
        functools.partial(_proj_res_ln_kernel, alpha=alpha),
        grid=(T // tm,),
        in_specs=[pl.BlockSpec((tm, K), lambda i: (i, 0)),
                  _resident((K, D)), _resident((1, D)),
                  pl.BlockSpec((tm, D), lambda i: (i, 0)),
                  _resident((1, D)), _resident((1, D))],
        out_specs=pl.BlockSpec((tm, D), lambda i: (i, 0)),
        out_shape=jax.ShapeDtypeStruct((T, D), F32),
        compiler_params=_params("parallel"),
        name="proj_res_ln",
    )(a, w, bias.reshape(1, D), x, g.reshape(1, D), b.reshape(1, D))


def _mlp_kernel(x_ref, wu_ref, wd_ref, g_ref, b_ref, o_ref, xb_ref, acc_ref, *, alpha):
    j = pl.program_id(1)

    @pl.when(j == 0)
    def _():
        x = x_ref[...]
        xb_ref[...] = x.astype(MXU_DTYPE)
        acc_ref[...] = alpha * x

    u = jnp.dot(xb_ref[...], wu_ref[...], preferred_element_type=F32)
    u = jnp.maximum(u, 0.0)
    u = (u * u).astype(MXU_DTYPE)
    acc_ref[...] += jnp.dot(u, wd_ref[...], preferred_element_type=F32)

    @pl.when(j == pl.num_programs(1) - 1)
    def _():
        o_ref[...] = _layer_norm(acc_ref[...], g_ref[...], b_ref[...])


def _mlp_res_ln(x, wu, wd, g, b, alpha):
    T, D = x.shape
    Hd = wu.shape[1]
    tm = _tile(T, 512)
    th = _tile(Hd, 1024)
    return pl.pallas_call(
        functools.partial(_mlp_kernel, alpha=alpha),
        grid=(T // tm, Hd // th),
        in_specs=[pl.BlockSpec((tm, D), lambda i, j: (i, 0)),
                  pl.BlockSpec((D, th), lambda i, j: (0, j)),
                  pl.BlockSpec((th, D), lambda i, j: (j, 0)),
                  _resident((1, D)), _resident((1, D))],
        out_specs=pl.BlockSpec((tm, D), lambda i, j: (i, 0)),
        out_shape=jax.ShapeDtypeStruct((T, D), F32),
        scratch_shapes=[pltpu.VMEM((tm, D), MXU_DTYPE), pltpu.VMEM((tm, D), F32)],
        compiler_params=_params("parallel", "arbitrary"),
        name="mlp_res_ln",
    )(x, wu, wd, g.reshape(1, D), b.reshape(1, D))


def _mla_proj_kernel(x_ref, wqa_ref, wkva_ref, wkr_ref, wkrr_ref, qg_ref, kvg_ref,
                     wqn_ref, wqr_ref, wqrr_ref, wkn_ref, wv_ref, cos_ref, sin_ref,
                     qn_ref, qr_ref, kn_ref, v_ref, kr_ref, *, heads, scale):
    xb = x_ref[...].astype(MXU_DTYPE)
    cos = cos_ref[...]
    sin = sin_ref[...]
    q_lat = jnp.dot(xb, wqa_ref[...], preferred_element_type=F32)
    kv_lat = jnp.dot(xb, wkva_ref[...], preferred_element_type=F32)
    kr = jnp.dot(xb, wkr_ref[...], preferred_element_type=F32)
    krr = jnp.dot(xb, wkrr_ref[...], preferred_element_type=F32)
    kr_ref[...] = (kr * cos + krr * sin).astype(kr_ref.dtype)

    qn = _rms_norm(q_lat, qg_ref[...]).astype(MXU_DTYPE)
    qn_ref[...] = (jnp.dot(qn, wqn_ref[...], preferred_element_type=F32) * scale).astype(qn_ref.dtype)
    qr = jnp.dot(qn, wqr_ref[...], preferred_element_type=F32)
    qrr = jnp.dot(qn, wqrr_ref[...], preferred_element_type=F32)
    for h in range(heads):
        sl = slice(h * LANES, (h + 1) * LANES)
        qr_ref[:, sl] = ((qr[:, sl] * cos + qrr[:, sl] * sin) * scale).astype(qr_ref.dtype)

    kvn = _rms_norm(kv_lat, kvg_ref[...]).astype(MXU_DTYPE)
    kn_ref[...] = jnp.dot(kvn, wkn_ref[...], preferred_element_type=F32).astype(kn_ref.dtype)
    v_ref[...] = jnp.dot(kvn, wv_ref[...], preferred_element_type=F32).astype(v_ref.dtype)


def _rot_half_cols(w):
    half = w.shape[-1] // 2
    return jnp.concatenate([-w[..., half:], w[..., :half]], -1)


def _pad_lanes(w, width):
    pad = [(0, 0)] * (w.ndim - 1) + [(0, width - w.shape[-1])]
    return jnp.pad(w, pad)


def _mla_project(x, w_in, q_norm, kv_norm, w_q_b, w_kv_b, S):
    T, D = x.shape
    q_rank = q_norm.shape[-1]
    kv_rank = kv_norm.shape[-1]
    H = w_q_b.shape[1] // (MLA_NOPE + MLA_ROPE)
    scale = (MLA_NOPE + MLA_ROPE) ** -0.5 * math.log2(math.e)

    wqa = w_in[:, :q_rank].astype(MXU_DTYPE)
    wkva = w_in[:, q_rank:q_rank + kv_rank].astype(MXU_DTYPE)
    wkr_raw = w_in[:, q_rank + kv_rank:]
    wkr = _pad_lanes(wkr_raw, LANES).astype(MXU_DTYPE)
    wkrr = _pad_lanes(_rot_half_cols(wkr_raw), LANES).astype(MXU_DTYPE)
    wq3 = w_q_b.reshape(q_rank, H, MLA_NOPE + MLA_ROPE)
    wqn = wq3[:, :, :MLA_NOPE].reshape(q_rank, H * MLA_NOPE).astype(MXU_DTYPE)
    wq_rope = wq3[:, :, MLA_NOPE:]
    wqr = _pad_lanes(wq_rope, LANES).reshape(q_rank, H * LANES).astype(MXU_DTYPE)
    wqrr = _pad_lanes(_rot_half_cols(wq_rope), LANES).reshape(q_rank, H * LANES).astype(MXU_DTYPE)
    wkv3 = w_kv_b.reshape(kv_rank, H, MLA_NOPE + MLA_V)
    wkn = wkv3[:, :, :MLA_NOPE].reshape(kv_rank, H * MLA_NOPE).astype(MXU_DTYPE)
    wv = wkv3[:, :, MLA_NOPE:].reshape(kv_rank, H * MLA_V).astype(MXU_DTYPE)

    half = MLA_ROPE // 2
    inv = ROPE_THETA ** (-jnp.arange(half, dtype=F32) / half)
    ang = jnp.arange(S, dtype=F32)[:, None] * inv[None, :]
    cos = _pad_lanes(jnp.concatenate([jnp.cos(ang), jnp.cos(ang)], -1), LANES)
    sin = _pad_lanes(jnp.concatenate([jnp.sin(ang), jnp.sin(ang)], -1), LANES)

    tm = _tile(S, 512)
    ns = S // tm
    row = lambda i: (i, 0)
    pos = lambda i: (i % ns, 0)
    wide = jax.ShapeDtypeStruct((T, H * LANES), MXU_DTYPE)
    outs = pl.pallas_call(
        functools.partial(_mla_proj_kernel, heads=H, scale=scale),
        grid=(T // tm,),
        in_specs=[pl.BlockSpec((tm, D), row),
                  _resident(wqa.shape), _resident(wkva.shape), _resident(wkr.shape),
                  _resident(wkrr.shape), _resident((1, q_rank)), _resident((1, kv_rank)),
                  _resident(wqn.shape), _resident(wqr.shape), _resident(wqrr.shape),
                  _resident(wkn.shape), _resident(wv.shape),
                  pl.BlockSpec((tm, LANES), pos), pl.BlockSpec((tm, LANES), pos)],
        out_specs=[pl.BlockSpec((tm, H * LANES), row)] * 4 + [pl.BlockSpec((tm, LANES), row)],
        out_shape=[wide, wide, wide, wide, jax.ShapeDtypeStruct((T, LANES), MXU_DTYPE)],
        compiler_params=_params("parallel"),
        name="mla_proj",
    )(x, wqa, wkva, wkr, wkrr, q_norm.reshape(1, -1), kv_norm.reshape(1, -1),
      wqn, wqr, wqrr, wkn, wv, cos, sin)
    return outs, H


def _flash_kernel(qn_ref, qr_ref, kn_ref, kr_ref, v_ref, o_ref, m_scr, l_scr, acc_scr,
                  *, sub, streams):
    blk = pl.program_id(2)
    q = jnp.concatenate([qn_ref[...], qr_ref[...]], -1)
    m_scr[...] = jnp.full(m_scr.shape, NEG_INF, F32)
    l_scr[...] = jnp.zeros(l_scr.shape, F32)
    acc_scr[...] = jnp.zeros(acc_scr.shape, F32)

    def kv_rows(j):
        return pl.ds(pl.multiple_of(j * sub, sub), sub)

    def keys(j):
        return jnp.concatenate([kn_ref[kv_rows(j), :], kr_ref[kv_rows(j), :]], -1)

    def softmax_pv(s, a, j, diagonal):
        rows = slice(a * sub, (a + 1) * sub)
        if diagonal:
            r = lax.broadcasted_iota(jnp.int32, s.shape, 0)
            c = lax.broadcasted_iota(jnp.int32, s.shape, 1)
            s = jnp.where(c <= r, s, NEG_INF)
        m_prev = m_scr[rows, :]
        m_new = jnp.maximum(m_prev, jnp.max(s, -1, keepdims=True))
        p = jnp.exp2(s - jnp.concatenate([m_new] * (sub // LANES), -1))
        corr = jnp.exp2(m_prev - m_new)
        m_scr[rows, :] = m_new
        l_scr[rows, :] = corr * l_scr[rows, :] + jnp.sum(p, -1, keepdims=True)
        acc_scr[rows, :] = corr * acc_scr[rows, :] + jnp.dot(
            p.astype(MXU_DTYPE), v_ref[kv_rows(j), :], preferred_element_type=F32)

    def body(j, carry):
        s_all = _mm_nt(q, keys(j))
        for a in range(streams):
            softmax_pv(s_all[a * sub:(a + 1) * sub], a, j, False)
        return carry

    lax.fori_loop(0, streams * blk, body, 0)
    for b in range(streams):
        j = streams * blk + b
        s_part = _mm_nt(q[b * sub:], keys(j))
        for a in range(b, streams):
            softmax_pv(s_part[(a - b) * sub:(a - b + 1) * sub], a, j, a == b)
    o_ref[...] = (acc_scr[...] / l_scr[...]).astype(o_ref.dtype)


def _flash_attention(qn, qr, kn, kr, v, B, S, H):
    T = B * S
    sub = _tile(S, 512)
    streams = _tile(S // sub, 4)
    tq = sub * streams
    nq = S // tq
    qmap = lambda b, h, i: (b * nq + i, h)
    kmap = lambda b, h, i: (b, h)
    stat = pltpu.VMEM((tq, LANES), F32)
    return pl.pallas_call(
        functools.partial(_flash_kernel, sub=sub, streams=streams),
        grid=(B, H, nq),
        in_specs=[pl.BlockSpec((tq, LANES), qmap), pl.BlockSpec((tq, LANES), qmap),
                  pl.BlockSpec((S, LANES), kmap),
                  pl.BlockSpec((S, LANES), lambda b, h, i: (b, 0)),
                  pl.BlockSpec((S, LANES), kmap)],
        out_specs=pl.BlockSpec((tq, LANES), qmap),
        out_shape=jax.ShapeDtypeStruct((T, H * MLA_V), MXU_DTYPE),
        scratch_shapes=[stat, stat, stat],
        compiler_params=_params("parallel", "parallel", "arbitrary"),
        name="mla_flash",
    )(qn, qr, kn, kr, v)


def _mla_layer(h, S, w_in, q_norm, kv_norm, w_q_b, w_kv_b, w_out, g, b, alpha):
    T, D = h.shape
    (qn, qr, kn, v, kr), H = _mla_project(h, w_in, q_norm, kv_norm, w_q_b, w_kv_b, S)
    o = _flash_attention(qn, qr, kn, kr, v, T // S, S, H)
    return _proj_res_ln(o, w_out.astype(MXU_DTYPE), jnp.zeros((D,), F32), h, g, b, alpha)


def _shift_delta(x, prev_rows, is_seq_start):
    prev = jnp.where(is_seq_start, 0.0, prev_rows[SUBLANES - 1:SUBLANES, :])
    shifted = pltpu.roll(x, 1, 0)
    first = lax.broadcasted_iota(jnp.int32, x.shape, 0) == 0
    return jnp.where(first, prev, shifted) - x


def _rwkv_rkv_kernel(x_ref, prev_ref, mix_ref, w_ref, o_ref, *, tiles_per_seq):
    i = pl.program_id(1)
    x = x_ref[...]
    xx = _shift_delta(x, prev_ref[...], i % tiles_per_seq == 0)
    xm = (x + xx * mix_ref[0]).astype(MXU_DTYPE)
    o_ref[0] = jnp.dot(xm, w_ref[0], preferred_element_type=F32)


def _rwkv_rkv(x, mix3, w3, S):
    T, D = x.shape
    tm = _tile(S, 512)
    per8 = tm // SUBLANES
    return pl.pallas_call(
        functools.partial(_rwkv_rkv_kernel, tiles_per_seq=S // tm),
        grid=(3, T // tm),
        in_specs=[pl.BlockSpec((tm, D), lambda n, i: (i, 0)),
                  pl.BlockSpec((SUBLANES, D), lambda n, i: (jnp.maximum(i * per8 - 1, 0), 0)),
                  pl.BlockSpec((1, 1, D), lambda n, i: (n, 0, 0)),
                  pl.BlockSpec((1, D, D), lambda n, i: (n, 0, 0))],
        out_specs=pl.BlockSpec((1, tm, D), lambda n, i: (n, i, 0)),
        out_shape=jax.ShapeDtypeStruct((3, T, D), F32),
        compiler_params=_params("parallel", "parallel"),
        name="rwkv_rkv",
    )(x, x, mix3, w3)


def _rwkv_lora_kernel(x_ref, prev_ref, mix_ref, w0_ref, a0_ref, w1_ref, w2_ref, a1_ref,
                      a2_ref, g1_ref, g2_ref, ld_ref, a_ref, g_ref, *, tiles_per_seq):
    i = pl.program_id(0)
    x = x_ref[...]
    xx = _shift_delta(x, prev_ref[...], i % tiles_per_seq == 0)
    xw = (x + xx * mix_ref[0:1, :]).astype(MXU_DTYPE)
    xa = (x + xx * mix_ref[1:2, :]).astype(MXU_DTYPE)
    xg = (x + xx * mix_ref[2:3, :]).astype(MXU_DTYPE)
    hw = jnp.tanh(jnp.dot(xw, w1_ref[...], preferred_element_type=F32)).astype(MXU_DTYPE)
    u = w0_ref[...] + jnp.dot(hw, w2_ref[...], preferred_element_type=F32)
    ld_ref[...] = -math.exp(-0.5) * _sigmoid(u)
    ha = jnp.dot(xa, a1_ref[...], preferred_element_type=F32).astype(MXU_DTYPE)
    a_ref[...] = _sigmoid(a0_ref[...] + jnp.dot(ha, a2_ref[...], preferred_element_type=F32))
    hg = _sigmoid(jnp.dot(xg, g1_ref[...], preferred_element_type=F32)).astype(MXU_DTYPE)
    g_ref[...] = jnp.dot(hg, g2_ref[...], preferred_element_type=F32)


def _pad_rank(w_down, w_up):
    r = w_down.shape[1]
    rp = -(-r // LANES) * LANES
    return (jnp.pad(w_down, ((0, 0), (0, rp - r))).astype(MXU_DTYPE),
            jnp.pad(w_up, ((0, rp - r), (0, 0))).astype(MXU_DTYPE))


def _rwkv_lora(x, mix3, w0, a0, w1, w2, a1, a2, g1, g2, S):
    T, D = x.shape
    tm = _tile(S, 512)
    per8 = tm // SUBLANES
    row = lambda i: (i, 0)
    out = jax.ShapeDtypeStruct((T, D), F32)
    w1p, w2p = _pad_rank(w1, w2)
    a1p, a2p = _pad_rank(a1, a2)
    g1p, g2p = _pad_rank(g1, g2)
    return pl.pallas_call(
        functools.partial(_rwkv_lora_kernel, tiles_per_seq=S // tm),
        grid=(T // tm,),
        in_specs=[pl.BlockSpec((tm, D), row),
                  pl.BlockSpec((SUBLANES, D), lambda i: (jnp.maximum(i * per8 - 1, 0), 0)),
                  _resident((3, D)), _resident((1, D)), _resident((1, D)),
                  _resident(w1p.shape), _resident(w2p.shape), _resident(a1p.shape),
                  _resident(a2p.shape), _resident(g1p.shape), _resident(g2p.shape)],
        out_specs=[pl.BlockSpec((tm, D), row)] * 3,
        out_shape=[out, out, out],
        compiler_params=_params("parallel"),
        name="rwkv_lora",
    )(x, x, mix3, w0.reshape(1, D), a0.reshape(1, D), w1p, w2p, a1p, a2p, g1p, g2p)


def _wkv_kernel(r_ref, k_ref, v_ref, ld_ref, a_ref, g_ref, kk_ref, ka_ref, rk_ref,
                lng_ref, lnb_ref, z_ref, state_ref, *, pairs, chunk):
    L = chunk
    c_idx = pl.program_id(2)

    @pl.when(c_idx == 0)
    def _():
        state_ref[...] = jnp.zeros_like(state_ref)

    lane = lax.broadcasted_iota(jnp.int32, (1, LANES), 1)
    head0 = lane < RWKV_HEAD

    def head_sum(t):
        s0 = jnp.sum(jnp.where(head0, t, 0.0), -1, keepdims=True)
        s1 = jnp.sum(jnp.where(head0, 0.0, t), -1, keepdims=True)
        return jnp.where(head0, s0, s1)

    def stack(t):
        return jnp.concatenate([jnp.where(head0, t, 0.0), jnp.where(head0, 0.0, t)], 0)

    ld_all = ld_ref[...]
    tri = (lax.broadcasted_iota(jnp.int32, (L, L), 0)
           >= lax.broadcasted_iota(jnp.int32, (L, L), 1)).astype(MXU_DTYPE)
    ld_hi = ld_all.astype(MXU_DTYPE)
    rem = ld_all - ld_hi.astype(F32)
    ld_mid = rem.astype(MXU_DTYPE)
    ld_lo = (rem - ld_mid.astype(F32)).astype(MXU_DTYPE)
    cum_all = (jnp.dot(tri, ld_hi, preferred_element_type=F32)
               + jnp.dot(tri, ld_mid, preferred_element_type=F32)
               + jnp.dot(tri, ld_lo, preferred_element_type=F32))

    row2 = lax.broadcasted_iota(jnp.int32, (2 * L, 2 * L), 0)
    col2 = lax.broadcasted_iota(jnp.int32, (2 * L, 2 * L), 1)
    same_head = (row2 // L) == (col2 // L)
    strict = same_head & ((row2 % L) > (col2 % L))
    incl = same_head & ((row2 % L) >= (col2 % L))
    eye = (row2 == col2).astype(F32)

    slices = [slice(p * LANES, (p + 1) * LANES) for p in range(pairs)]
    lhs, rhs, v2, tails, w_chunk = [], [], [], [], []
    for sl in slices:
        r = r_ref[:, sl]
        k = k_ref[:, sl]
        a = a_ref[:, sl]
        ld = ld_all[:, sl]
        cum = cum_all[:, sl]
        kk = k * kk_ref[:, sl]
        kk = kk / jnp.maximum(jnp.sqrt(head_sum(kk * kk)), 1e-12)
        k_mod = k * (1.0 + (a - 1.0) * ka_ref[:, sl])
        bb = kk * a
        cum_last = cum[L - 1:L, :]
        e_neg = jnp.exp(-cum)
        e_tail = jnp.exp(cum_last - cum)
        lhs.append(jnp.concatenate([stack(-kk * jnp.exp(cum - ld)), stack(r * jnp.exp(cum))],
                                   0).astype(MXU_DTYPE))
        rhs.append(jnp.concatenate([stack(bb * e_neg), stack(k_mod * e_neg)], 0).astype(MXU_DTYPE))
        tails.append(jnp.concatenate([stack(bb * e_tail), stack(k_mod * e_tail)],
                                     0).astype(MXU_DTYPE))
        v2.append(stack(v_ref[:, sl]).astype(MXU_DTYPE))
        w_chunk.append(jnp.exp(cum_last))

    gram = [_mm_nt(lhs[p], rhs[p]) for p in range(pairs)]
    n_ab = [jnp.where(strict, g_[:2 * L, :2 * L], 0.0) for g_ in gram]
    m_ak = [jnp.where(strict, g_[:2 * L, 2 * L:], 0.0).astype(MXU_DTYPE) for g_ in gram]
    n_r = [jnp.concatenate([jnp.where(incl, g_[2 * L:, :2 * L], 0.0),
                            jnp.where(incl, g_[2 * L:, 2 * L:], 0.0)], 1).astype(MXU_DTYPE)
           for g_ in gram]

    levels = int(math.log2(L))
    t_inv = [eye + n for n in n_ab]
    n_pow = [_mm(n, n) for n in n_ab]
    for _ in range(levels - 2):
        both = [_mm(jnp.concatenate([n_pow[p].astype(MXU_DTYPE), t_inv[p].astype(MXU_DTYPE)], 0),
                    n_pow[p]) for p in range(pairs)]
        n_pow = [b_[:2 * L] for b_ in both]
        t_inv = [t_inv[p] + both[p][2 * L:] for p in range(pairs)]
    t_inv = [t_inv[p] + _mm(t_inv[p], n_pow[p]) for p in range(pairs)]

    state = [state_ref[p] for p in range(pairs)]
    from_state = [_mm_nt(lhs[p], state[p]) for p in range(pairs)]
    x = [from_state[p][:2 * L] + _mm(m_ak[p], v2[p]) for p in range(pairs)]
    sa_v = [jnp.concatenate([_mm(t_inv[p], x[p]).astype(MXU_DTYPE), v2[p]], 0)
            for p in range(pairs)]
    y2 = [from_state[p][2 * L:] + _mm(n_r[p], sa_v[p]) for p in range(pairs)]
    for p in range(pairs):
        state_ref[p] = state[p] * w_chunk[p] + _mm_tn(sa_v[p], tails[p])

    for p, sl in enumerate(slices):
        y = y2[p][:L] + y2[p][L:]
        r = r_ref[:, sl]
        k_mod = k_ref[:, sl] * (1.0 + (a_ref[:, sl] - 1.0) * ka_ref[:, sl])
        mu = head_sum(y) * (1.0 / RWKV_HEAD)
        d = y - mu
        var = head_sum(d * d) * (1.0 / RWKV_HEAD)
        yn = d * lax.rsqrt(var + RWKV_GN_EPS) * lng_ref[:, sl] + lnb_ref[:, sl]
        bonus = head_sum(r * k_mod * rk_ref[:, sl]) * v_ref[:, sl]
        z_ref[:, sl] = ((yn + bonus) * g_ref[:, sl]).astype(z_ref.dtype)


def _wkv(rkv, ld, a, g, k_k, k_a, r_k, ln_g, ln_b, B, S):
    _, T, D = rkv.shape
    L = RWKV_CHUNK
    n_pairs = D // LANES
    pairs = _tile(n_pairs, 16)
    width = pairs * LANES
    nc = S // L
    tok = lambda b, q, c: (b * nc + c, q)
    par = lambda b, q, c: (0, q)

    def rkv_spec(n):
        return pl.BlockSpec((None, L, width), lambda b, q, c: (n, b * nc + c, q))

    vec = lambda t: t.reshape(1, D)
    return pl.pallas_call(
        functools.partial(_wkv_kernel, pairs=pairs, chunk=L),
        grid=(B, n_pairs // pairs, nc),
        in_specs=[rkv_spec(0), rkv_spec(1), rkv_spec(2),
                  pl.BlockSpec((L, width), tok), pl.BlockSpec((L, width), tok),
                  pl.BlockSpec((L, width), tok)] + [pl.BlockSpec((1, width), par)] * 5,
        out_specs=pl.BlockSpec((L, width), tok),
        out_shape=jax.ShapeDtypeStruct((T, D), MXU_DTYPE),
        scratch_shapes=[pltpu.VMEM((pairs, LANES, LANES), F32)],
        compiler_params=_params("parallel", "parallel", "arbitrary"),
        name="wkv7",
    )(rkv, rkv, rkv, ld, a, g, vec(k_k), vec(k_a), vec(r_k), vec(ln_g), vec(ln_b))


def _rwkv_layer(h, S, mix, w_in, w0, w1, w2, a0, a1, a2, g1, g2, k_k, k_a, r_k,
                ln_g, ln_b, w_out, g, b, alpha):
    T, D = h.shape
    rkv = _rwkv_rkv(h, mix[jnp.array([0, 2, 3])].reshape(3, 1, D), w_in.astype(MXU_DTYPE), S)
    ld, a, gate = _rwkv_lora(h, mix[jnp.array([1, 4, 5])], w0, a0, w1, w2, a1, a2, g1, g2, S)
    z = _wkv(rkv, ld, a, gate, k_k, k_a, r_k, ln_g, ln_b, T // S, S)
    return _proj_res_ln(z, w_out.astype(MXU_DTYPE), jnp.zeros((D,), F32), h, g, b, alpha)


def _swa_qkv_kernel(x_ref, w_ref, b_ref, q_ref, k_ref, v_ref, *, q_width, kv_width, scale):
    y = jnp.dot(x_ref[...].astype(MXU_DTYPE), w_ref[...], preferred_element_type=F32) + b_ref[...]
    q_ref[...] = (y[:, :q_width] * scale).astype(q_ref.dtype)
    k_ref[...] = y[:, q_width:q_width + kv_width].astype(k_ref.dtype)
    v_ref[...] = y[:, q_width + kv_width:].astype(v_ref.dtype)


def _swa_qkv(x, w_in, b_in, Hq, Hk):
    T, D = x.shape
    Dh = SWA_HEAD
    qw = Hq * Dh

    def dup(t):
        t3 = t[..., qw:].reshape(t.shape[:-1] + (2, Hk, Dh))
        return jnp.concatenate([t3, t3], -1).reshape(t.shape[:-1] + (2 * Hk * 2 * Dh,))

    w = jnp.concatenate([w_in[:, :qw], dup(w_in)], -1).astype(MXU_DTYPE)
    bias = jnp.concatenate([b_in[:qw], dup(b_in)], -1).reshape(1, -1)
    kvw = Hk * 2 * Dh
    tm = _tile(T, 512)
    row = lambda i: (i, 0)
    return pl.pallas_call(
        functools.partial(_swa_qkv_kernel, q_width=qw, kv_width=kvw, scale=Dh ** -0.5),
        grid=(T // tm,),
        in_specs=[pl.BlockSpec((tm, D), row), _resident(w.shape), _resident(bias.shape)],
        out_specs=[pl.BlockSpec((tm, qw), row), pl.BlockSpec((tm, kvw), row),
                   pl.BlockSpec((tm, kvw), row)],
        out_shape=[jax.ShapeDtypeStruct((T, qw), MXU_DTYPE),
                   jax.ShapeDtypeStruct((T, kvw), MXU_DTYPE),
                   jax.ShapeDtypeStruct((T, kvw), MXU_DTYPE)],
        compiler_params=_params("parallel"),
        name="swa_qkv",
    )(x, w, bias)


def _swa_bias_kernel(bucket_ref, window_ref, rel_ref, sinks_ref, o_ref, sink_rows_ref, *, buckets):
    h = pl.program_id(0)
    bucket = bucket_ref[...]
    acc = jnp.zeros(bucket.shape, F32)
    for n in range(buckets):
        acc = jnp.where(bucket == n, rel_ref[n, h], acc)
    o_ref[0] = jnp.where(window_ref[...] > 0, acc, NEG_INF)
    sink_rows_ref[0] = jnp.full(sink_rows_ref.shape[1:], sinks_ref[h], F32)


def _t5_bucket(dist):
    max_exact = REL_BUCKETS // 2
    n = jnp.maximum(dist, 0)
    nf = jnp.maximum(n, 1).astype(F32)
    large = max_exact + (jnp.log(nf / max_exact) / math.log(SWA_WINDOW / max_exact)
                         * (REL_BUCKETS - max_exact)).astype(jnp.int32)
    large = jnp.minimum(large, REL_BUCKETS - 1)
    return jnp.where(n < max_exact, n, large)


def _swa_bias(rel_bias, sinks):
    W = SWA_WINDOW
    Hq = rel_bias.shape[1]
    dist = jnp.arange(W)[:, None] + W - jnp.arange(2 * W)[None, :]
    bucket = _t5_bucket(dist).astype(jnp.int32)
    in_window = ((dist >= 0) & (dist < W)).astype(jnp.int32)
    return pl.pallas_call(
        functools.partial(_swa_bias_kernel, buckets=REL_BUCKETS),
        grid=(Hq,),
        in_specs=[_resident((W, 2 * W)), _resident((W, 2 * W)),
                  pl.BlockSpec(memory_space=pltpu.SMEM), pl.BlockSpec(memory_space=pltpu.SMEM)],
        out_specs=[pl.BlockSpec((1, W, 2 * W), lambda h: (h, 0, 0)),
                   pl.BlockSpec((1, W, LANES), lambda h: (h, 0, 0))],
        out_shape=[jax.ShapeDtypeStruct((Hq, W, 2 * W), F32),
                   jax.ShapeDtypeStruct((Hq, W, LANES), F32)],
        compiler_params=_params("arbitrary"),
        name="swa_bias",
    )(bucket, in_window, rel_bias, sinks)


def _swa_attn_kernel(q_ref, kc_ref, kp_ref, vc_ref, vp_ref, bias_ref, sink_ref, o_ref,
                     *, kv_heads, group):
    W = SWA_WINDOW
    i = pl.program_id(1)
    lane = lax.broadcasted_iota(jnp.int32, (1, LANES), 1)
    half0 = lane < SWA_HEAD
    rows = group * W
    col = lax.broadcasted_iota(jnp.int32, (rows, 2 * W), 1)
    key_ok = (col >= W) | (i > 0)
    scores = []
    for kv in range(kv_heads):
        ksl = slice(kv * LANES, (kv + 1) * LANES)
        k_band = jnp.concatenate([kp_ref[:, ksl], kc_ref[:, ksl]], 0)
        parts = []
        for pair in range(group // 2):
            blk = kv * (group // 2) + pair
            qp = q_ref[:, blk * LANES:(blk + 1) * LANES]
            zero = jnp.zeros_like(qp)
            parts += [jnp.where(half0, qp, zero), jnp.where(half0, zero, qp)]
        scores.append(_mm_nt(jnp.concatenate(parts, 0), k_band))
    for kv in range(kv_heads):
        ksl = slice(kv * LANES, (kv + 1) * LANES)
        v_band = jnp.concatenate([vp_ref[:, ksl], vc_ref[:, ksl]], 0)
        heads = slice(kv * group, (kv + 1) * group)
        s = scores[kv] + bias_ref[heads].reshape(rows, 2 * W)
        s = jnp.where(key_ok, s, NEG_INF)
        sink = sink_ref[heads].reshape(rows, LANES)
        m = jnp.maximum(jnp.max(s, -1, keepdims=True), sink)
        p = jnp.exp(s - jnp.concatenate([m] * (2 * W // LANES), -1))
        denom = jnp.sum(p, -1, keepdims=True) + jnp.exp(sink - m)
        o = jnp.dot(p.astype(MXU_DTYPE), v_band, preferred_element_type=F32) / denom
        for pair in range(group // 2):
            blk = kv * (group // 2) + pair
            o0 = o[(2 * pair) * W:(2 * pair + 1) * W]
            o1 = o[(2 * pair + 1) * W:(2 * pair + 2) * W]
            o_ref[:, blk * LANES:(blk + 1) * LANES] = jnp.where(half0, o0, o1).astype(o_ref.dtype)


def _swa_attention(q, kd, vd, bias, sink_rows, B, S, Hq, Hk):
    T = B * S
    W = SWA_WINDOW
    nb = S // W
    qw = Hq * SWA_HEAD
    kvw = Hk * LANES
    cur = lambda b, i: (b * nb + i, 0)
    prev = lambda b, i: (b * nb + jnp.maximum(i - 1, 0), 0)
    return pl.pallas_call(
        functools.partial(_swa_attn_kernel, kv_heads=Hk, group=Hq // Hk),
        grid=(B, nb),
        in_specs=[pl.BlockSpec((W, qw), cur),
                  pl.BlockSpec((W, kvw), cur), pl.BlockSpec((W, kvw), prev),
                  pl.BlockSpec((W, kvw), cur), pl.BlockSpec((W, kvw), prev),
                  _resident(bias.shape), _resident(sink_rows.shape)],
        out_specs=pl.BlockSpec((W, qw), cur),
        out_shape=jax.ShapeDtypeStruct((T, qw), MXU_DTYPE),
        compiler_params=_params("parallel", "arbitrary"),
        name="swa_attn",
    )(q, kd, kd, vd, vd, bias, sink_rows)


def _swa_layer(h, S, w_in, b_in, sinks, w_out, b_out, rel_bias, g, b, alpha):
    T, D = h.shape
    Hq = D // SWA_HEAD
    Hk = (w_in.shape[1] - D) // (2 * SWA_HEAD)
    assert (Hq // Hk) % 2 == 0
    q, kd, vd = _swa_qkv(h, w_in, b_in, Hq, Hk)
    bias, sink_rows = _swa_bias(rel_bias, sinks)
    o = _swa_attention(q, kd, vd, bias, sink_rows, T // S, S, Hq, Hk)
    return _proj_res_ln(o, w_out.astype(MXU_DTYPE), b_out, h, g, b, alpha)


def kernel(x, mla_w_in, mla_q_norm, mla_kv_norm, mla_w_q_b, mla_w_kv_b, mla_w_out, rwkv_mix, rwkv_w_in, rwkv_w0, rwkv_w1, rwkv_w2, rwkv_a0, rwkv_a1, rwkv_a2, rwkv_g1, rwkv_g2, rwkv_k_k, rwkv_k_a, rwkv_r_k, rwkv_ln_g, rwkv_ln_b, rwkv_w_out, swa_w_in, swa_b_in, swa_sinks, swa_w_out, swa_b_out, rel_bias, ln_g, ln_b, mlp_up, mlp_down):
    B, S, D = x.shape
    depth = ln_g.shape[0]
    alpha = (2.0 * depth) ** 0.25
    h = x.reshape(B * S, D)
    for i in range(depth):
        j = i // 3
        kind = i % 3
        g0, b0 = ln_g[i, 0], ln_b[i, 0]
        if kind == 0:
            h = _mla_layer(h, S, mla_w_in[j], mla_q_norm[j], mla_kv_norm[j], mla_w_q_b[j],
                           mla_w_kv_b[j], mla_w_out[j], g0, b0, alpha)
        elif kind == 1:
            h = _rwkv_layer(h, S, rwkv_mix[j], rwkv_w_in[j], rwkv_w0[j], rwkv_w1[j], rwkv_w2[j],
                            rwkv_a0[j], rwkv_a1[j], rwkv_a2[j], rwkv_g1[j], rwkv_g2[j],
                            rwkv_k_k[j], rwkv_k_a[j], rwkv_r_k[j].reshape(-1), rwkv_ln_g[j],
                            rwkv_ln_b[j], rwkv_w_out[j], g0, b0, alpha)
        else:
            h = _swa_layer(h, S, swa_w_in[j], swa_b_in[j], swa_sinks[j], swa_w_out[j],
                           swa_b_out[j], rel_bias, g0, b0, alpha)
        h = _mlp_res_ln(h, mlp_up[i].astype(MXU_DTYPE), mlp_down[i].astype(MXU_DTYPE),
                        ln_g[i, 1], ln_b[i, 1], alpha)
    return h.reshape(B, S, D)
```

```python
import functools
import math

import jax
import jax.numpy as jnp
from jax import lax
from jax.experimental import pallas as pl
from jax.experimental.pallas import tpu as pltpu

F32 = jnp.float32
MXU_DTYPE = jnp.bfloat16

LN_EPS = 1e-5
RMS_EPS = 1e-6
NEG_INF = -1e30
ROPE_THETA = 10000.0

LANES = 128
SUBLANES = 8
VMEM_LIMIT = 56 * 1024 * 1024

MLA_NOPE = 128
MLA_ROPE = 64
MLA_V = 128
RWKV_HEAD = 64
RWKV_GN_EPS = 64e-5
RWKV_CHUNK = 64
SWA_HEAD = 64
SWA_WINDOW = 128
REL_BUCKETS = 32


def _params(*sem):
    return pltpu.CompilerParams(dimension_semantics=sem, vmem_limit_bytes=VMEM_LIMIT)


def _tile(n, want):
    t = min(n, want)
    assert n % t == 0, (n, t)
    return t


def _resident(shape):
    nd = len(shape)
    return pl.BlockSpec(shape, lambda *_: (0,) * nd, pipeline_mode=pl.Buffered(1))


def _mm(a, b):
    return jnp.dot(a.astype(MXU_DTYPE), b.astype(MXU_DTYPE), preferred_element_type=F32)


def _mm_nt(a, b):
    return lax.dot_general(a.astype(MXU_DTYPE), b.astype(MXU_DTYPE),
                           (((1,), (1,)), ((), ())), preferred_element_type=F32)


def _mm_tn(a, b):
    return lax.dot_general(a.astype(MXU_DTYPE), b.astype(MXU_DTYPE),
                           (((0,), (0,)), ((), ())), preferred_element_type=F32)


def _split(a):
    hi = a.astype(MXU_DTYPE)
    lo = (a - hi.astype(F32)).astype(MXU_DTYPE)
    return hi, lo


def _mm3(a, b):
    ah, al = _split(a)
    bh, bl = _split(b)
    return (jnp.dot(ah, bh, preferred_element_type=F32)
            + jnp.dot(ah, bl, preferred_element_type=F32)
            + jnp.dot(al, bh, preferred_element_type=F32))


def _layer_norm(y, g, b):
    mu = jnp.mean(y, -1, keepdims=True)
    d = y - mu
    var = jnp.mean(d * d, -1, keepdims=True)
    return d * lax.rsqrt(var + LN_EPS) * g + b


def _rms_norm(y, g):
    return y * lax.rsqrt(jnp.mean(y * y, -1, keepdims=True) + RMS_EPS) * g


def _sigmoid(z):
    return 1.0 / (1.0 + jnp.exp(-z))


def _proj_res_ln_kernel(a_ref, w_ref, bias_ref, x_ref, g_ref, b_ref, o_ref, *, alpha):
    y = jnp.dot(a_ref[...], w_ref[...], preferred_element_type=F32)
    y = y + bias_ref[...] + alpha * x_ref[...]
    o_ref[...] = _layer_norm(y, g_ref[...], b_ref[...])


def _proj_res_ln(a, w, bias, x, g, b, alpha):
    T, K = a.shape
    D = w.shape[1]
    tm = _tile(T, 512)
    return pl.pallas_call(
        functools.partial(_proj_res_ln_kernel, alpha=alpha),
        grid=(T // tm,),
        in_specs=[pl.BlockSpec((tm, K), lambda i: (i, 0)),
                  _resident((K, D)), _resident((1, D)),
                  pl.BlockSpec((tm, D), lambda i: (i, 0)),
                  _resident((1, D)), _resident((1, D))],
        out_specs=pl.BlockSpec((tm, D), lambda i: (i, 0)),
        out_shape=jax.ShapeDtypeStruct((T, D), F32),
        compiler_params=_params("parallel"),
        name="proj_res_ln",
    )(a, w, bias.reshape(1, D), x, g.reshape(1, D), b.reshape(1, D))


def _mlp_kernel(x_ref, wu_ref, wd_ref, g_ref, b_ref, o_ref, xb_ref, acc_ref, *, alpha):
    j = pl.program_id(1)

    @pl.when(j == 0)
    def _():
        x = x_ref[...]
        xb = x.astype(MXU_DTYPE)
        xb_ref[...] = xb
        u = jnp.maximum(jnp.dot(xb, wu_ref[...], preferred_element_type=F32), 0.0)
        u = (u * u).astype(MXU_DTYPE)
        acc_ref[...] = alpha * x + jnp.dot(u, wd_ref[...], preferred_element_type=F32)

    @pl.when(j > 0)
    def _():
        u = jnp.maximum(jnp.dot(xb_ref[...], wu_ref[...], preferred_element_type=F32), 0.0)
        u = (u * u).astype(MXU_DTYPE)
        acc_ref[...] += jnp.dot(u, wd_ref[...], preferred_element_type=F32)

    @pl.when(j == pl.num_programs(1) - 1)
    def _():
        o_ref[...] = _layer_norm(acc_ref[...], g_ref[...], b_ref[...])


def _mlp_res_ln(x, wu_all, wd_all, layer, g, b, alpha):
    T, D = x.shape
    Hd = wu_all.shape[2]
    tm = _tile(T, 512)
    th = _tile(Hd, 1024)
    return pl.pallas_call(
        functools.partial(_mlp_kernel, alpha=alpha),
        grid=(T // tm, Hd // th),
        in_specs=[pl.BlockSpec((tm, D), lambda i, j: (i, 0)),
                  pl.BlockSpec((None, D, th), lambda i, j: (layer, 0, j)),
                  pl.BlockSpec((None, th, D), lambda i, j: (layer, j, 0)),
                  _resident((1, D)), _resident((1, D))],
        out_specs=pl.BlockSpec((tm, D), lambda i, j: (i, 0)),
        out_shape=jax.ShapeDtypeStruct((T, D), F32),
        scratch_shapes=[pltpu.VMEM((tm, D), MXU_DTYPE), pltpu.VMEM((tm, D), F32)],
        compiler_params=_params("parallel", "arbitrary"),
        name="mlp_res_ln",
    )(x, wu_all, wd_all, g.reshape(1, D), b.reshape(1, D))


def _mla_proj_kernel(x_ref, wqa_ref, wkva_ref, wkr_ref, wkrr_ref, qg_ref, kvg_ref,
                     wqn_ref, wqr_ref, wqrr_ref, wkn_ref, wv_ref, cos_ref, sin_ref,
                     qn_ref, qr_ref, kn_ref, v_ref, kr_ref, *, heads, scale):
    xb = x_ref[...].astype(MXU_DTYPE)
    cos = cos_ref[...]
    sin = sin_ref[...]
    q_lat = jnp.dot(xb, wqa_ref[...], preferred_element_type=F32)
    kv_lat = jnp.dot(xb, wkva_ref[...], preferred_element_type=F32)
    kr = jnp.dot(xb, wkr_ref[...], preferred_element_type=F32)
    krr = jnp.dot(xb, wkrr_ref[...], preferred_element_type=F32)
    kr_ref[...] = (kr * cos + krr * sin).astype(kr_ref.dtype)

    qn = _rms_norm(q_lat, qg_ref[...]).astype(MXU_DTYPE)
    qn_ref[...] = (jnp.dot(qn, wqn_ref[...], preferred_element_type=F32) * scale).astype(qn_ref.dtype)
    qr = jnp.dot(qn, wqr_ref[...], preferred_element_type=F32)
    qrr = jnp.dot(qn, wqrr_ref[...], preferred_element_type=F32)
    for h in range(heads):
        sl = slice(h * LANES, (h + 1) * LANES)
        qr_ref[:, sl] = ((qr[:, sl] * cos + qrr[:, sl] * sin) * scale).astype(qr_ref.dtype)

    kvn = _rms_norm(kv_lat, kvg_ref[...]).astype(MXU_DTYPE)
    kn_ref[...] = jnp.dot(kvn, wkn_ref[...], preferred_element_type=F32).astype(kn_ref.dtype)
    v_ref[...] = jnp.dot(kvn, wv_ref[...], preferred_element_type=F32).astype(v_ref.dtype)


def _rot_half_cols(w):
    half = w.shape[-1] // 2
    return jnp.concatenate([-w[..., half:], w[..., :half]], -1)


def _pad_lanes(w, width):
    pad = [(0, 0)] * (w.ndim - 1) + [(0, width - w.shape[-1])]
    return jnp.pad(w, pad)


def _mla_project(x, w_in, q_norm, kv_norm, w_q_b, w_kv_b, S):
    T, D = x.shape
    q_rank = q_norm.shape[-1]
    kv_rank = kv_norm.shape[-1]
    H = w_q_b.shape[1] // (MLA_NOPE + MLA_ROPE)
    scale = (MLA_NOPE + MLA_ROPE) ** -0.5 * math.log2(math.e)

    wqa = w_in[:, :q_rank].astype(MXU_DTYPE)
    wkva = w_in[:, q_rank:q_rank + kv_rank].astype(MXU_DTYPE)
    wkr_raw = w_in[:, q_rank + kv_rank:]
    wkr = _pad_lanes(wkr_raw, LANES).astype(MXU_DTYPE)
    wkrr = _pad_lanes(_rot_half_cols(wkr_raw), LANES).astype(MXU_DTYPE)
    wq3 = w_q_b.reshape(q_rank, H, MLA_NOPE + MLA_ROPE)
    wqn = wq3[:, :, :MLA_NOPE].reshape(q_rank, H * MLA_NOPE).astype(MXU_DTYPE)
    wq_rope = wq3[:, :, MLA_NOPE:]
    wqr = _pad_lanes(wq_rope, LANES).reshape(q_rank, H * LANES).astype(MXU_DTYPE)
    wqrr = _pad_lanes(_rot_half_cols(wq_rope), LANES).reshape(q_rank, H * LANES).astype(MXU_DTYPE)
    wkv3 = w_kv_b.reshape(kv_rank, H, MLA_NOPE + MLA_V)
    wkn = wkv3[:, :, :MLA_NOPE].reshape(kv_rank, H * MLA_NOPE).astype(MXU_DTYPE)
    wv = wkv3[:, :, MLA_NOPE:].reshape(kv_rank, H * MLA_V).astype(MXU_DTYPE)

    half = MLA_ROPE // 2
    inv = ROPE_THETA ** (-jnp.arange(half, dtype=F32) / half)
    ang = jnp.arange(S, dtype=F32)[:, None] * inv[None, :]
    cos = _pad_lanes(jnp.concatenate([jnp.cos(ang), jnp.cos(ang)], -1), LANES)
    sin = _pad_lanes(jnp.concatenate([jnp.sin(ang), jnp.sin(ang)], -1), LANES)

    tm = _tile(S, 512)
    ns = S // tm
    row = lambda i: (i, 0)
    pos = lambda i: (i % ns, 0)
    wide = jax.ShapeDtypeStruct((T, H * LANES), MXU_DTYPE)
    outs = pl.pallas_call(
        functools.partial(_mla_proj_kernel, heads=H, scale=scale),
        grid=(T // tm,),
        in_specs=[pl.BlockSpec((tm, D), row),
                  _resident(wqa.shape), _resident(wkva.shape), _resident(wkr.shape),
                  _resident(wkrr.shape), _resident((1, q_rank)), _resident((1, kv_rank)),
                  _resident(wqn.shape), _resident(wqr.shape), _resident(wqrr.shape),
                  _resident(wkn.shape), _resident(wv.shape),
                  pl.BlockSpec((tm, LANES), pos), pl.BlockSpec((tm, LANES), pos)],
        out_specs=[pl.BlockSpec((tm, H * LANES), row)] * 4 + [pl.BlockSpec((tm, LANES), row)],
        out_shape=[wide, wide, wide, wide, jax.ShapeDtypeStruct((T, LANES), MXU_DTYPE)],
        compiler_params=_params("parallel"),
        name="mla_proj",
    )(x, wqa, wkva, wkr, wkrr, q_norm.reshape(1, -1), kv_norm.reshape(1, -1),
      wqn, wqr, wqrr, wkn, wv, cos, sin)
    return outs, H


def _flash_kernel(qn_ref, qr_ref, kn_ref, kr_ref, v_ref, o_ref, m_scr, l_scr, acc_scr,
                  *, sub, streams):
    blk = pl.program_id(2)
    q = jnp.concatenate([qn_ref[...], qr_ref[...]], -1)
    m_scr[...] = jnp.full(m_scr.shape, NEG_INF, F32)
    l_scr[...] = jnp.zeros(l_scr.shape, F32)
    acc_scr[...] = jnp.zeros(acc_scr.shape, F32)

    def kv_rows(j):
        return pl.ds(pl.multiple_of(j * sub, sub), sub)

    def keys(j):
        return jnp.concatenate([kn_ref[kv_rows(j), :], kr_ref[kv_rows(j), :]], -1)

    def softmax_pv(s, a, j, diagonal):
        rows = slice(a * sub, (a + 1) * sub)
        if diagonal:
            r = lax.broadcasted_iota(jnp.int32, s.shape, 0)
            c = lax.broadcasted_iota(jnp.int32, s.shape, 1)
            s = jnp.where(c <= r, s, NEG_INF)
        m_prev = m_scr[rows, :]
        m_new = jnp.maximum(m_prev, jnp.max(s, -1, keepdims=True))
        p = jnp.exp2(s - jnp.concatenate([m_new] * (sub // LANES), -1))
        corr = jnp.exp2(m_prev - m_new)
        m_scr[rows, :] = m_new
        l_scr[rows, :] = corr * l_scr[rows, :] + jnp.sum(p, -1, keepdims=True)
        acc_scr[rows, :] = corr * acc_scr[rows, :] + jnp.dot(
            p.astype(MXU_DTYPE), v_ref[kv_rows(j), :], preferred_element_type=F32)

    def body(j, carry):
        k = keys(j)
        s_next = _mm_nt(q[:sub], k)
        for a in range(streams):
            s = s_next
            if a + 1 < streams:
                s_next = _mm_nt(q[(a + 1) * sub:(a + 2) * sub], k)
            softmax_pv(s, a, j, False)
        return carry

    lax.fori_loop(0, streams * blk, body, 0)
    for b in range(streams):
        j = streams * blk + b
        s_part = _mm_nt(q[b * sub:], keys(j))
        for a in range(b, streams):
            softmax_pv(s_part[(a - b) * sub:(a - b + 1) * sub], a, j, a == b)
    o_ref[...] = (acc_scr[...] / l_scr[...]).astype(o_ref.dtype)


def _flash_attention(qn, qr, kn, kr, v, B, S, H):
    T = B * S
    sub = _tile(S, 512)
    streams = _tile(S // sub, 8)
    tq = sub * streams
    nq = S // tq
    qmap = lambda b, h, i: (b * nq + i, h)
    kmap = lambda b, h, i: (b, h)
    stat = pltpu.VMEM((tq, LANES), F32)
    return pl.pallas_call(
        functools.partial(_flash_kernel, sub=sub, streams=streams),
        grid=(B, H, nq),
        in_specs=[pl.BlockSpec((tq, LANES), qmap), pl.BlockSpec((tq, LANES), qmap),
                  pl.BlockSpec((S, LANES), kmap),
                  pl.BlockSpec((S, LANES), lambda b, h, i: (b, 0)),
                  pl.BlockSpec((S, LANES), kmap)],
        out_specs=pl.BlockSpec((tq, LANES), qmap),
        out_shape=jax.ShapeDtypeStruct((T, H * MLA_V), MXU_DTYPE),
        scratch_shapes=[stat, stat, stat],
        compiler_params=_params("parallel", "parallel", "arbitrary"),
        name="mla_flash",
    )(qn, qr, kn, kr, v)


def _mla_layer(h, S, w_in, q_norm, kv_norm, w_q_b, w_kv_b, w_out, g, b, alpha):
    T, D = h.shape
    (qn, qr, kn, v, kr), H = _mla_project(h, w_in, q_norm, kv_norm, w_q_b, w_kv_b, S)
    o = _flash_attention(qn, qr, kn, kr, v, T // S, S, H)
    return _proj_res_ln(o, w_out.astype(MXU_DTYPE), jnp.zeros((D,), F32), h, g, b, alpha)


def _shift_delta(x, prev_rows, is_seq_start):
    prev = jnp.where(is_seq_start, 0.0, prev_rows[SUBLANES - 1:SUBLANES, :])
    shifted = pltpu.roll(x, 1, 0)
    first = lax.broadcasted_iota(jnp.int32, x.shape, 0) == 0
    return jnp.where(first, prev, shifted) - x


def _rwkv_rkv_kernel(x_ref, prev_ref, mix_ref, w_ref, o_ref, *, tiles_per_seq):
    i = pl.program_id(1)
    x = x_ref[...]
    xx = _shift_delta(x, prev_ref[...], i % tiles_per_seq == 0)
    xm = (x + xx * mix_ref[0]).astype(MXU_DTYPE)
    o_ref[0] = jnp.dot(xm, w_ref[0], preferred_element_type=F32)


def _rwkv_rkv(x, mix3, w3, S):
    T, D = x.shape
    tm = _tile(S, 512)
    per8 = tm // SUBLANES
    return pl.pallas_call(
        functools.partial(_rwkv_rkv_kernel, tiles_per_seq=S // tm),
        grid=(3, T // tm),
        in_specs=[pl.BlockSpec((tm, D), lambda n, i: (i, 0)),
                  pl.BlockSpec((SUBLANES, D), lambda n, i: (jnp.maximum(i * per8 - 1, 0), 0)),
                  pl.BlockSpec((1, 1, D), lambda n, i: (n, 0, 0)),
                  pl.BlockSpec((1, D, D), lambda n, i: (n, 0, 0))],
        out_specs=pl.BlockSpec((1, tm, D), lambda n, i: (n, i, 0)),
        out_shape=jax.ShapeDtypeStruct((3, T, D), F32),
        compiler_params=_params("parallel", "parallel"),
        name="rwkv_rkv",
    )(x, x, mix3, w3)


def _rwkv_lora_kernel(x_ref, prev_ref, mix_ref, w0_ref, a0_ref, w1_ref, w2_ref, a1_ref,
                      a2_ref, g1_ref, g2_ref, ld_ref, a_ref, g_ref, *, tiles_per_seq):
    i = pl.program_id(0)
    x = x_ref[...]
    xx = _shift_delta(x, prev_ref[...], i % tiles_per_seq == 0)
    xw = (x + xx * mix_ref[0:1, :]).astype(MXU_DTYPE)
    xa = (x + xx * mix_ref[1:2, :]).astype(MXU_DTYPE)
    xg = (x + xx * mix_ref[2:3, :]).astype(MXU_DTYPE)
    hw = jnp.tanh(jnp.dot(xw, w1_ref[...], preferred_element_type=F32)).astype(MXU_DTYPE)
    u = w0_ref[...] + jnp.dot(hw, w2_ref[...], preferred_element_type=F32)
    ld_ref[...] = -math.exp(-0.5) * _sigmoid(u)
    ha = jnp.dot(xa, a1_ref[...], preferred_element_type=F32).astype(MXU_DTYPE)
    a_ref[...] = _sigmoid(a0_ref[...] + jnp.dot(ha, a2_ref[...], preferred_element_type=F32))
    hg = _sigmoid(jnp.dot(xg, g1_ref[...], preferred_element_type=F32)).astype(MXU_DTYPE)
    g_ref[...] = jnp.dot(hg, g2_ref[...], preferred_element_type=F32)


def _pad_rank(w_down, w_up):
    r = w_down.shape[1]
    rp = -(-r // LANES) * LANES
    return (jnp.pad(w_down, ((0, 0), (0, rp - r))).astype(MXU_DTYPE),
            jnp.pad(w_up, ((0, rp - r), (0, 0))).astype(MXU_DTYPE))


def _rwkv_lora(x, mix3, w0, a0, w1, w2, a1, a2, g1, g2, S):
    T, D = x.shape
    tm = _tile(S, 512)
    per8 = tm // SUBLANES
    row = lambda i: (i, 0)
    out = jax.ShapeDtypeStruct((T, D), F32)
    w1p, w2p = _pad_rank(w1, w2)
    a1p, a2p = _pad_rank(a1, a2)
    g1p, g2p = _pad_rank(g1, g2)
    return pl.pallas_call(
        functools.partial(_rwkv_lora_kernel, tiles_per_seq=S // tm),
        grid=(T // tm,),
        in_specs=[pl.BlockSpec((tm, D), row),
                  pl.BlockSpec((SUBLANES, D), lambda i: (jnp.maximum(i * per8 - 1, 0), 0)),
                  _resident((3, D)), _resident((1, D)), _resident((1, D)),
                  _resident(w1p.shape), _resident(w2p.shape), _resident(a1p.shape),
                  _resident(a2p.shape), _resident(g1p.shape), _resident(g2p.shape)],
        out_specs=[pl.BlockSpec((tm, D), row)] * 3,
        out_shape=[out, out, out],
        compiler_params=_params("parallel"),
        name="rwkv_lora",
    )(x, x, mix3, w0.reshape(1, D), a0.reshape(1, D), w1p, w2p, a1p, a2p, g1p, g2p)


def _wkv_kernel(r_ref, k_ref, v_ref, ld_ref, a_ref, g_ref, kk_ref, ka_ref, rk_ref,
                lng_ref, lnb_ref, z_ref, state_ref, *, pairs, chunk):
    L = chunk
    c_idx = pl.program_id(2)

    @pl.when(c_idx == 0)
    def _():
        state_ref[...] = jnp.zeros_like(state_ref)

    lane = lax.broadcasted_iota(jnp.int32, (1, LANES), 1)
    head0 = lane < RWKV_HEAD

    def head_sum(t):
        s0 = jnp.sum(jnp.where(head0, t, 0.0), -1, keepdims=True)
        s1 = jnp.sum(jnp.where(head0, 0.0, t), -1, keepdims=True)
        return jnp.where(head0, s0, s1)

    def stack(t):
        return jnp.concatenate([jnp.where(head0, t, 0.0), jnp.where(head0, 0.0, t)], 0)

    ld_all = ld_ref[...]
    tri = (lax.broadcasted_iota(jnp.int32, (L, L), 0)
           >= lax.broadcasted_iota(jnp.int32, (L, L), 1)).astype(MXU_DTYPE)
    ld_hi = ld_all.astype(MXU_DTYPE)
    rem = ld_all - ld_hi.astype(F32)
    ld_mid = rem.astype(MXU_DTYPE)
    ld_lo = (rem - ld_mid.astype(F32)).astype(MXU_DTYPE)
    cum_all = (jnp.dot(tri, ld_hi, preferred_element_type=F32)
               + jnp.dot(tri, ld_mid, preferred_element_type=F32)
               + jnp.dot(tri, ld_lo, preferred_element_type=F32))

    row2 = lax.broadcasted_iota(jnp.int32, (2 * L, 2 * L), 0)
    col2 = lax.broadcasted_iota(jnp.int32, (2 * L, 2 * L), 1)
    same_head = (row2 // L) == (col2 // L)
    strict = same_head & ((row2 % L) > (col2 % L))
    incl = same_head & ((row2 % L) >= (col2 % L))
    eye = (row2 == col2).astype(F32)

    slices = [slice(p * LANES, (p + 1) * LANES) for p in range(pairs)]
    lhs, rhs, v2, tails, w_chunk = [], [], [], [], []
    for sl in slices:
        r = r_ref[:, sl]
        k = k_ref[:, sl]
        a = a_ref[:, sl]
        ld = ld_all[:, sl]
        cum = cum_all[:, sl]
        kk = k * kk_ref[:, sl]
        kk = kk / jnp.maximum(jnp.sqrt(head_sum(kk * kk)), 1e-12)
        k_mod = k * (1.0 + (a - 1.0) * ka_ref[:, sl])
        bb = kk * a
        cum_last = cum[L - 1:L, :]
        e_neg = jnp.exp(-cum)
        e_tail = jnp.exp(cum_last - cum)
        lhs.append(jnp.concatenate([stack(-kk * jnp.exp(cum - ld)), stack(r * jnp.exp(cum))],
                                   0).astype(MXU_DTYPE))
        rhs.append(jnp.concatenate([stack(bb * e_neg), stack(k_mod * e_neg)], 0).astype(MXU_DTYPE))
        tails.append(jnp.concatenate([stack(bb * e_tail), stack(k_mod * e_tail)],
                                     0).astype(MXU_DTYPE))
        v2.append(stack(v_ref[:, sl]).astype(MXU_DTYPE))
        w_chunk.append(jnp.exp(cum_last))

    gram = [_mm_nt(lhs[p], rhs[p]) for p in range(pairs)]
    n_ab = [jnp.where(strict, g_[:2 * L, :2 * L], 0.0) for g_ in gram]
    m_ak = [jnp.where(strict, g_[:2 * L, 2 * L:], 0.0).astype(MXU_DTYPE) for g_ in gram]
    n_r = [jnp.concatenate([jnp.where(incl, g_[2 * L:, :2 * L], 0.0),
                            jnp.where(incl, g_[2 * L:, 2 * L:], 0.0)], 1).astype(MXU_DTYPE)
           for g_ in gram]

    levels = int(math.log2(L))
    t_inv = [eye + n for n in n_ab]
    n_pow = [_mm(n, n) for n in n_ab]
    for _ in range(levels - 2):
        both = [_mm(jnp.concatenate([n_pow[p].astype(MXU_DTYPE), t_inv[p].astype(MXU_DTYPE)], 0),
                    n_pow[p]) for p in range(pairs)]
        n_pow = [b_[:2 * L] for b_ in both]
        t_inv = [t_inv[p] + both[p][2 * L:] for p in range(pairs)]
    t_inv = [t_inv[p] + _mm(t_inv[p], n_pow[p]) for p in range(pairs)]

    state = [state_ref[p] for p in range(pairs)]
    from_state = [_mm_nt(lhs[p], state[p]) for p in range(pairs)]
    x = [from_state[p][:2 * L] + _mm(m_ak[p], v2[p]) for p in range(pairs)]
    sa_v = [jnp.concatenate([_mm(t_inv[p], x[p]).astype(MXU_DTYPE), v2[p]], 0)
            for p in range(pairs)]
    y2 = [from_state[p][2 * L:] + _mm(n_r[p], sa_v[p]) for p in range(pairs)]
    for p in range(pairs):
        state_ref[p] = state[p] * w_chunk[p] + _mm_tn(sa_v[p], tails[p])

    for p, sl in enumerate(slices):
        y = y2[p][:L] + y2[p][L:]
        r = r_ref[:, sl]
        k_mod = k_ref[:, sl] * (1.0 + (a_ref[:, sl] - 1.0) * ka_ref[:, sl])
        mu = head_sum(y) * (1.0 / RWKV_HEAD)
        d = y - mu
        var = head_sum(d * d) * (1.0 / RWKV_HEAD)
        yn = d * lax.rsqrt(var + RWKV_GN_EPS) * lng_ref[:, sl] + lnb_ref[:, sl]
        bonus = head_sum(r * k_mod * rk_ref[:, sl]) * v_ref[:, sl]
        z_ref[:, sl] = ((yn + bonus) * g_ref[:, sl]).astype(z_ref.dtype)


def _wkv(rkv, ld, a, g, k_k, k_a, r_k, ln_g, ln_b, B, S):
    _, T, D = rkv.shape
    L = RWKV_CHUNK
    n_pairs = D // LANES
    pairs = _tile(n_pairs, 16)
    width = pairs * LANES
    nc = S // L
    tok = lambda b, q, c: (b * nc + c, q)
    par = lambda b, q, c: (0, q)

    def rkv_spec(n):
        return pl.BlockSpec((None, L, width), lambda b, q, c: (n, b * nc + c, q))

    vec = lambda t: t.reshape(1, D)
    return pl.pallas_call(
        functools.partial(_wkv_kernel, pairs=pairs, chunk=L),
        grid=(B, n_pairs // pairs, nc),
        in_specs=[rkv_spec(0), rkv_spec(1), rkv_spec(2),
                  pl.BlockSpec((L, width), tok), pl.BlockSpec((L, width), tok),
                  pl.BlockSpec((L, width), tok)] + [pl.BlockSpec((1, width), par)] * 5,
        out_specs=pl.BlockSpec((L, width), tok),
        out_shape=jax.ShapeDtypeStruct((T, D), MXU_DTYPE),
        scratch_shapes=[pltpu.VMEM((pairs, LANES, LANES), F32)],
        compiler_params=_params("parallel", "parallel", "arbitrary"),
        name="wkv7",
    )(rkv, rkv, rkv, ld, a, g, vec(k_k), vec(k_a), vec(r_k), vec(ln_g), vec(ln_b))


def _rwkv_layer(h, S, mix, w_in, w0, w1, w2, a0, a1, a2, g1, g2, k_k, k_a, r_k,
                ln_g, ln_b, w_out, g, b, alpha):
    T, D = h.shape
    rkv = _rwkv_rkv(h, mix[jnp.array([0, 2, 3])].reshape(3, 1, D), w_in.astype(MXU_DTYPE), S)
    ld, a, gate = _rwkv_lora(h, mix[jnp.array([1, 4, 5])], w0, a0, w1, w2, a1, a2, g1, g2, S)
    z = _wkv(rkv, ld, a, gate, k_k, k_a, r_k, ln_g, ln_b, T // S, S)
    return _proj_res_ln(z, w_out.astype(MXU_DTYPE), jnp.zeros((D,), F32), h, g, b, alpha)


def _swa_qkv_kernel(x_ref, w_ref, b_ref, q_ref, k_ref, v_ref, *, q_width, kv_width, scale):
    y = jnp.dot(x_ref[...].astype(MXU_DTYPE), w_ref[...], preferred_element_type=F32) + b_ref[...]
    q_ref[...] = (y[:, :q_width] * scale).astype(q_ref.dtype)
    k_ref[...] = y[:, q_width:q_width + kv_width].astype(k_ref.dtype)
    v_ref[...] = y[:, q_width + kv_width:].astype(v_ref.dtype)


def _swa_qkv(x, w_in, b_in, Hq, Hk):
    T, D = x.shape
    Dh = SWA_HEAD
    qw = Hq * Dh

    def dup(t):
        t3 = t[..., qw:].reshape(t.shape[:-1] + (2, Hk, Dh))
        return jnp.concatenate([t3, t3], -1).reshape(t.shape[:-1] + (2 * Hk * 2 * Dh,))

    w = jnp.concatenate([w_in[:, :qw], dup(w_in)], -1).astype(MXU_DTYPE)
    bias = jnp.concatenate([b_in[:qw], dup(b_in)], -1).reshape(1, -1)
    kvw = Hk * 2 * Dh
    tm = _tile(T, 512)
    row = lambda i: (i, 0)
    return pl.pallas_call(
        functools.partial(_swa_qkv_kernel, q_width=qw, kv_width=kvw, scale=Dh ** -0.5),
        grid=(T // tm,),
        in_specs=[pl.BlockSpec((tm, D), row), _resident(w.shape), _resident(bias.shape)],
        out_specs=[pl.BlockSpec((tm, qw), row), pl.BlockSpec((tm, kvw), row),
                   pl.BlockSpec((tm, kvw), row)],
        out_shape=[jax.ShapeDtypeStruct((T, qw), MXU_DTYPE),
                   jax.ShapeDtypeStruct((T, kvw), MXU_DTYPE),
                   jax.ShapeDtypeStruct((T, kvw), MXU_DTYPE)],
        compiler_params=_params("parallel"),
        name="swa_qkv",
    )(x, w, bias)


def _swa_bias_kernel(bucket_ref, window_ref, rel_ref, sinks_ref, o_ref, sink_rows_ref, *, buckets):
    h = pl.program_id(0)
    bucket = bucket_ref[...]
    acc = jnp.zeros(bucket.shape, F32)
    for n in range(buckets):
        acc = jnp.where(bucket == n, rel_ref[n, h], acc)
    o_ref[0] = jnp.where(window_ref[...] > 0, acc, NEG_INF)
    sink_rows_ref[0] = jnp.full(sink_rows_ref.shape[1:], sinks_ref[h], F32)


def _t5_bucket(dist):
    max_exact = REL_BUCKETS // 2
    n = jnp.maximum(dist, 0)
    nf = jnp.maximum(n, 1).astype(F32)
    large = max_exact + (jnp.log(nf / max_exact) / math.log(SWA_WINDOW / max_exact)
                         * (REL_BUCKETS - max_exact)).astype(jnp.int32)
    large = jnp.minimum(large, REL_BUCKETS - 1)
    return jnp.where(n < max_exact, n, large)


def _swa_bias(rel_bias, sinks):
    W = SWA_WINDOW
    Hq = rel_bias.shape[1]
    dist = jnp.arange(W)[:, None] + W - jnp.arange(2 * W)[None, :]
    bucket = _t5_bucket(dist).astype(jnp.int32)
    in_window = ((dist >= 0) & (dist < W)).astype(jnp.int32)
    return pl.pallas_call(
        functools.partial(_swa_bias_kernel, buckets=REL_BUCKETS),
        grid=(Hq,),
        in_specs=[_resident((W, 2 * W)), _resident((W, 2 * W)),
                  pl.BlockSpec(memory_space=pltpu.SMEM), pl.BlockSpec(memory_space=pltpu.SMEM)],
        out_specs=[pl.BlockSpec((1, W, 2 * W), lambda h: (h, 0, 0)),
                   pl.BlockSpec((1, W, LANES), lambda h: (h, 0, 0))],
        out_shape=[jax.ShapeDtypeStruct((Hq, W, 2 * W), F32),
                   jax.ShapeDtypeStruct((Hq, W, LANES), F32)],
        compiler_params=_params("arbitrary"),
        name="swa_bias",
    )(bucket, in_window, rel_bias, sinks)


def _swa_attn_kernel(q_ref, kc_ref, kp_ref, vc_ref, vp_ref, bias_ref, sink_ref, o_ref,
                     *, kv_heads, group):
    W = SWA_WINDOW
    i = pl.program_id(1)
    lane = lax.broadcasted_iota(jnp.int32, (1, LANES), 1)
    half0 = lane < SWA_HEAD
    rows = group * W
    col = lax.broadcasted_iota(jnp.int32, (rows, 2 * W), 1)
    key_ok = (col >= W) | (i > 0)
    scores = []
    for kv in range(kv_heads):
        ksl = slice(kv * LANES, (kv + 1) * LANES)
        k_band = jnp.concatenate([kp_ref[:, ksl], kc_ref[:, ksl]], 0)
        parts = []
        for pair in range(group // 2):
            blk = kv * (group // 2) + pair
            qp = q_ref[:, blk * LANES:(blk + 1) * LANES]
            zero = jnp.zeros_like(qp)
            parts += [jnp.where(half0, qp, zero), jnp.where(half0, zero, qp)]
        scores.append(_mm_nt(jnp.concatenate(parts, 0), k_band))
    for kv in range(kv_heads):
        ksl = slice(kv * LANES, (kv + 1) * LANES)
        v_band = jnp.concatenate([vp_ref[:, ksl], vc_ref[:, ksl]], 0)
        heads = slice(kv * group, (kv + 1) * group)
        s = scores[kv] + bias_ref[heads].reshape(rows, 2 * W)
        s = jnp.where(key_ok, s, NEG_INF)
        sink = sink_ref[heads].reshape(rows, LANES)
        m = jnp.maximum(jnp.max(s, -1, keepdims=True), sink)
        p = jnp.exp(s - jnp.concatenate([m] * (2 * W // LANES), -1))
        denom = jnp.sum(p, -1, keepdims=True) + jnp.exp(sink - m)
        o = jnp.dot(p.astype(MXU_DTYPE), v_band, preferred_element_type=F32) / denom
        for pair in range(group // 2):
            blk = kv * (group // 2) + pair
            o0 = o[(2 * pair) * W:(2 * pair + 1) * W]
            o1 = o[(2 * pair + 1) * W:(2 * pair + 2) * W]
            o_ref[:, blk * LANES:(blk + 1) * LANES] = jnp.where(half0, o0, o1).astype(o_ref.dtype)


def _swa_attention(q, kd, vd, bias, sink_rows, B, S, Hq, Hk):
    T = B * S
    W = SWA_WINDOW
    nb = S // W
    qw = Hq * SWA_HEAD
    kvw = Hk * LANES
    cur = lambda b, i: (b * nb + i, 0)
    prev = lambda b, i: (b * nb + jnp.maximum(i - 1, 0), 0)
    return pl.pallas_call(
        functools.partial(_swa_attn_kernel, kv_heads=Hk, group=Hq // Hk),
        grid=(B, nb),
        in_specs=[pl.BlockSpec((W, qw), cur),
                  pl.BlockSpec((W, kvw), cur), pl.BlockSpec((W, kvw), prev),
                  pl.BlockSpec((W, kvw), cur), pl.BlockSpec((W, kvw), prev),
                  _resident(bias.shape), _resident(sink_rows.shape)],
        out_specs=pl.BlockSpec((W, qw), cur),
        out_shape=jax.ShapeDtypeStruct((T, qw), MXU_DTYPE),
        compiler_params=_params("parallel", "arbitrary"),
        name="swa_attn",
    )(q, kd, kd, vd, vd, bias, sink_rows)


def _swa_layer(h, S, w_in, b_in, sinks, w_out, b_out, rel_bias, g, b, alpha):
    T, D = h.shape
    Hq = D // SWA_HEAD
    Hk = (w_in.shape[1] - D) // (2 * SWA_HEAD)
    assert (Hq // Hk) % 2 == 0
    q, kd, vd = _swa_qkv(h, w_in, b_in, Hq, Hk)
    bias, sink_rows = _swa_bias(rel_bias, sinks)
    o = _swa_attention(q, kd, vd, bias, sink_rows, T // S, S, Hq, Hk)
    return _proj_res_ln(o, w_out.astype(MXU_DTYPE), b_out, h, g, b, alpha)


def kernel(x, mla_w_in, mla_q_norm, mla_kv_norm, mla_w_q_b, mla_w_kv_b, mla_w_out, rwkv_mix, rwkv_w_in, rwkv_w0, rwkv_w1, rwkv_w2, rwkv_a0, rwkv_a1, rwkv_a2, rwkv_g1, rwkv_g2, rwkv_k_k, rwkv_k_a, rwkv_r_k, rwkv_ln_g, rwkv_ln_b, rwkv_w_out, swa_w_in, swa_b_in, swa_sinks, swa_w_out, swa_b_out, rel_bias, ln_g, ln_b, mlp_up, mlp_down):
    B, S, D = x.shape
    depth = ln_g.shape[0]
    alpha = (2.0 * depth) ** 0.25
    h = x.reshape(B * S, D)
    mlp_up_b = mlp_up.astype(MXU_DTYPE)
    mlp_down_b = mlp_down.astype(MXU_DTYPE)
    for i in range(depth):
        j = i // 3
        kind = i % 3
        g0, b0 = ln_g[i, 0], ln_b[i, 0]
        if kind == 0:
            h = _mla_layer(h, S, mla_w_in[j], mla_q_norm[j], mla_kv_norm[j], mla_w_q_b[j],
                           mla_w_kv_b[j], mla_w_out[j], g0, b0, alpha)
        elif kind == 1:
            h = _rwkv_layer(h, S, rwkv_mix[j], rwkv_w_in[j], rwkv_w0[j], rwkv_w1[j], rwkv_w2[j],
                            rwkv_a0[j], rwkv_a1[j], rwkv_a2[j], rwkv_g1[j], rwkv_g2[j],
                            rwkv_k_k[j], rwkv_k_a[j], rwkv_r_k[j].reshape(-1), rwkv_ln_g[j],
                            rwkv_ln_b[j], rwkv_w_out[j], g0, b0, alpha)
        else:
            h = _swa_layer(h, S, swa_w_in[j], swa_b_in[j], swa_sinks[j], swa_w_out[j],
                           swa_b_out[j], rel_bias, g0, b0, alpha)
        h = _mlp_res_ln(h, mlp_up_b, mlp_down_b, i, ln_g[i, 1], ln_b[i, 1], alpha)
    return h.reshape(B, S, D)
```

```python
import functools
import math

import jax
import jax.numpy as jnp
from jax import lax
from jax.experimental import pallas as pl
from jax.experimental.pallas import tpu as pltpu

F32 = jnp.float32
MXU_DTYPE = jnp.bfloat16

LN_EPS = 1e-5
RMS_EPS = 1e-6
NEG_INF = -1e30
ROPE_THETA = 10000.0

LANES = 128
SUBLANES = 8
VMEM_LIMIT = 56 * 1024 * 1024

MLA_NOPE = 128
MLA_ROPE = 64
MLA_V = 128
RWKV_HEAD = 64
RWKV_GN_EPS = 64e-5
RWKV_CHUNK = 64
SWA_HEAD = 64
SWA_WINDOW = 128
REL_BUCKETS = 32


def _params(*sem):
    return pltpu.CompilerParams(dimension_semantics=sem, vmem_limit_bytes=VMEM_LIMIT)


def _tile(n, want):
    t = min(n, want)
    assert n % t == 0, (n, t)
    return t


def _resident(shape):
    nd = len(shape)
    return pl.BlockSpec(shape, lambda *_: (0,) * nd, pipeline_mode=pl.Buffered(1))


def _mm(a, b):
    return jnp.dot(a.astype(MXU_DTYPE), b.astype(MXU_DTYPE), preferred_element_type=F32)


def _mm_nt(a, b):
    return lax.dot_general(a.astype(MXU_DTYPE), b.astype(MXU_DTYPE),
                           (((1,), (1,)), ((), ())), preferred_element_type=F32)


def _mm_tn(a, b):
    return lax.dot_general(a.astype(MXU_DTYPE), b.astype(MXU_DTYPE),
                           (((0,), (0,)), ((), ())), preferred_element_type=F32)


def _split(a):
    hi = a.astype(MXU_DTYPE)
    lo = (a - hi.astype(F32)).astype(MXU_DTYPE)
    return hi, lo


def _mm3(a, b):
    ah, al = _split(a)
    bh, bl = _split(b)
    return (jnp.dot(ah, bh, preferred_element_type=F32)
            + jnp.dot(ah, bl, preferred_element_type=F32)
            + jnp.dot(al, bh, preferred_element_type=F32))


def _layer_norm(y, g, b):
    mu = jnp.mean(y, -1, keepdims=True)
    d = y - mu
    var = jnp.mean(d * d, -1, keepdims=True)
    return d * lax.rsqrt(var + LN_EPS) * g + b


def _rms_norm(y, g):
    return y * lax.rsqrt(jnp.mean(y * y, -1, keepdims=True) + RMS_EPS) * g


def _sigmoid(z):
    return 1.0 / (1.0 + jnp.exp(-z))


def _proj_res_ln_kernel(a_ref, w_ref, bias_ref, x_ref, g_ref, b_ref, o_ref, *, alpha):
    y = jnp.dot(a_ref[...], w_ref[...], preferred_element_type=F32)
    y = y + bias_ref[...] + alpha * x_ref[...]
    o_ref[...] = _layer_norm(y, g_ref[...], b_ref[...])


def _proj_res_ln(a, w, bias, x, g, b, alpha):
    T, K = a.shape
    D = w.shape[1]
    tm = _tile(T, 512)
    return pl.pallas_call(
        functools.partial(_proj_res_ln_kernel, alpha=alpha),
        grid=(T // tm,),
        in_specs=[pl.BlockSpec((tm, K), lambda i: (i, 0)),
                  _resident((K, D)), _resident((1, D)),
                  pl.BlockSpec((tm, D), lambda i: (i, 0)),
                  _resident((1, D)), _resident((1, D))],
        out_specs=pl.BlockSpec((tm, D), lambda i: (i, 0)),
        out_shape=jax.ShapeDtypeStruct((T, D), F32),
        compiler_params=_params("parallel"),
        name="proj_res_ln",
    )(a, w, bias.reshape(1, D), x, g.reshape(1, D), b.reshape(1, D))


def _mlp_kernel(x_ref, wu_ref, wd_ref, g_ref, b_ref, o_ref, xb_ref, acc_ref, *, alpha):
    j = pl.program_id(1)

    @pl.when(j == 0)
    def _():
        x = x_ref[...]
        xb = x.astype(MXU_DTYPE)
        xb_ref[...] = xb
        u = jnp.maximum(jnp.dot(xb, wu_ref[...], preferred_element_type=F32), 0.0)
        u = (u * u).astype(MXU_DTYPE)
        acc_ref[...] = alpha * x + jnp.dot(u, wd_ref[...], preferred_element_type=F32)

    @pl.when(j > 0)
    def _():
        u = jnp.maximum(jnp.dot(xb_ref[...], wu_ref[...], preferred_element_type=F32), 0.0)
        u = (u * u).astype(MXU_DTYPE)
        acc_ref[...] += jnp.dot(u, wd_ref[...], preferred_element_type=F32)

    @pl.when(j == pl.num_programs(1) - 1)
    def _():
        o_ref[...] = _layer_norm(acc_ref[...], g_ref[...], b_ref[...])


def _mlp_res_ln(x, wu_all, wd_all, layer, g, b, alpha):
    T, D = x.shape
    Hd = wu_all.shape[2]
    tm = _tile(T, 512)
    th = _tile(Hd, 1024)
    return pl.pallas_call(
        functools.partial(_mlp_kernel, alpha=alpha),
        grid=(T // tm, Hd // th),
        in_specs=[pl.BlockSpec((tm, D), lambda i, j: (i, 0)),
                  pl.BlockSpec((None, D, th), lambda i, j: (layer, 0, j)),
                  pl.BlockSpec((None, th, D), lambda i, j: (layer, j, 0)),
                  _resident((1, D)), _resident((1, D))],
        out_specs=pl.BlockSpec((tm, D), lambda i, j: (i, 0)),
        out_shape=jax.ShapeDtypeStruct((T, D), F32),
        scratch_shapes=[pltpu.VMEM((tm, D), MXU_DTYPE), pltpu.VMEM((tm, D), F32)],
        compiler_params=_params("parallel", "arbitrary"),
        name="mlp_res_ln",
    )(x, wu_all, wd_all, g.reshape(1, D), b.reshape(1, D))


def _mla_proj_kernel(x_ref, wqa_ref, wkva_ref, wkr_ref, wkrr_ref, qg_ref, kvg_ref,
                     wqn_ref, wqr_ref, wqrr_ref, wkn_ref, wv_ref, cos_ref, sin_ref,
                     qn_ref, qr_ref, kn_ref, v_ref, kr_ref, *, heads, scale):
    xb = x_ref[...].astype(MXU_DTYPE)
    cos = cos_ref[...]
    sin = sin_ref[...]
    q_lat = jnp.dot(xb, wqa_ref[...], preferred_element_type=F32)
    kv_lat = jnp.dot(xb, wkva_ref[...], preferred_element_type=F32)
    kr = jnp.dot(xb, wkr_ref[...], preferred_element_type=F32)
    krr = jnp.dot(xb, wkrr_ref[...], preferred_element_type=F32)
    kr_ref[...] = (kr * cos + krr * sin).astype(kr_ref.dtype)

    qn = _rms_norm(q_lat, qg_ref[...]).astype(MXU_DTYPE)
    qn_ref[...] = (jnp.dot(qn, wqn_ref[...], preferred_element_type=F32) * scale).astype(qn_ref.dtype)
    qr = jnp.dot(qn, wqr_ref[...], preferred_element_type=F32)
    qrr = jnp.dot(qn, wqrr_ref[...], preferred_element_type=F32)
    for h in range(heads):
        sl = slice(h * LANES, (h + 1) * LANES)
        qr_ref[:, sl] = ((qr[:, sl] * cos + qrr[:, sl] * sin) * scale).astype(qr_ref.dtype)

    kvn = _rms_norm(kv_lat, kvg_ref[...]).astype(MXU_DTYPE)
    kn_ref[...] = jnp.dot(kvn, wkn_ref[...], preferred_element_type=F32).astype(kn_ref.dtype)
    v_ref[...] = jnp.dot(kvn, wv_ref[...], preferred_element_type=F32).astype(v_ref.dtype)


def _rot_half_cols(w):
    half = w.shape[-1] // 2
    return jnp.concatenate([-w[..., half:], w[..., :half]], -1)


def _pad_lanes(w, width):
    pad = [(0, 0)] * (w.ndim - 1) + [(0, width - w.shape[-1])]
    return jnp.pad(w, pad)


def _mla_project(x, w_in, q_norm, kv_norm, w_q_b, w_kv_b, S):
    T, D = x.shape
    q_rank = q_norm.shape[-1]
    kv_rank = kv_norm.shape[-1]
    H = w_q_b.shape[1] // (MLA_NOPE + MLA_ROPE)
    scale = (MLA_NOPE + MLA_ROPE) ** -0.5 * math.log2(math.e)

    wqa = w_in[:, :q_rank].astype(MXU_DTYPE)
    wkva = w_in[:, q_rank:q_rank + kv_rank].astype(MXU_DTYPE)
    wkr_raw = w_in[:, q_rank + kv_rank:]
    wkr = _pad_lanes(wkr_raw, LANES).astype(MXU_DTYPE)
    wkrr = _pad_lanes(_rot_half_cols(wkr_raw), LANES).astype(MXU_DTYPE)
    wq3 = w_q_b.reshape(q_rank, H, MLA_NOPE + MLA_ROPE)
    wqn = wq3[:, :, :MLA_NOPE].reshape(q_rank, H * MLA_NOPE).astype(MXU_DTYPE)
    wq_rope = wq3[:, :, MLA_NOPE:]
    wqr = _pad_lanes(wq_rope, LANES).reshape(q_rank, H * LANES).astype(MXU_DTYPE)
    wqrr = _pad_lanes(_rot_half_cols(wq_rope), LANES).reshape(q_rank, H * LANES).astype(MXU_DTYPE)
    wkv3 = w_kv_b.reshape(kv_rank, H, MLA_NOPE + MLA_V)
    wkn = wkv3[:, :, :MLA_NOPE].reshape(kv_rank, H * MLA_NOPE).astype(MXU_DTYPE)
    wv = wkv3[:, :, MLA_NOPE:].reshape(kv_rank, H * MLA_V).astype(MXU_DTYPE)

    half = MLA_ROPE // 2
    inv = ROPE_THETA ** (-jnp.arange(half, dtype=F32) / half)
    ang = jnp.arange(S, dtype=F32)[:, None] * inv[None, :]
    cos = _pad_lanes(jnp.concatenate([jnp.cos(ang), jnp.cos(ang)], -1), LANES)
    sin = _pad_lanes(jnp.concatenate([jnp.sin(ang), jnp.sin(ang)], -1), LANES)

    tm = _tile(S, 512)
    ns = S // tm
    row = lambda i: (i, 0)
    pos = lambda i: (i % ns, 0)
    wide = jax.ShapeDtypeStruct((T, H * LANES), MXU_DTYPE)
    outs = pl.pallas_call(
        functools.partial(_mla_proj_kernel, heads=H, scale=scale),
        grid=(T // tm,),
        in_specs=[pl.BlockSpec((tm, D), row),
                  _resident(wqa.shape), _resident(wkva.shape), _resident(wkr.shape),
                  _resident(wkrr.shape), _resident((1, q_rank)), _resident((1, kv_rank)),
                  _resident(wqn.shape), _resident(wqr.shape), _resident(wqrr.shape),
                  _resident(wkn.shape), _resident(wv.shape),
                  pl.BlockSpec((tm, LANES), pos), pl.BlockSpec((tm, LANES), pos)],
        out_specs=[pl.BlockSpec((tm, H * LANES), row)] * 4 + [pl.BlockSpec((tm, LANES), row)],
        out_shape=[wide, wide, wide, wide, jax.ShapeDtypeStruct((T, LANES), MXU_DTYPE)],
        compiler_params=_params("parallel"),
        name="mla_proj",
    )(x, wqa, wkva, wkr, wkrr, q_norm.reshape(1, -1), kv_norm.reshape(1, -1),
      wqn, wqr, wqrr, wkn, wv, cos, sin)
    return outs, H


def _flash_kernel(qn_ref, qr_ref, kn_ref, kr_ref, v_ref, o_ref, m_scr, l_scr, acc_scr,
                  *, sub, streams):
    blk = pl.program_id(2)
    q = jnp.concatenate([qn_ref[...], qr_ref[...]], -1)
    m_scr[...] = jnp.full(m_scr.shape, NEG_INF, F32)
    l_scr[...] = jnp.zeros(l_scr.shape, F32)
    acc_scr[...] = jnp.zeros(acc_scr.shape, F32)

    def kv_rows(j):
        return pl.ds(pl.multiple_of(j * sub, sub), sub)

    def keys(j):
        return jnp.concatenate([kn_ref[kv_rows(j), :], kr_ref[kv_rows(j), :]], -1)

    def softmax_pv(s, a, j, diagonal):
        rows = slice(a * sub, (a + 1) * sub)
        if diagonal:
            r = lax.broadcasted_iota(jnp.int32, s.shape, 0)
            c = lax.broadcasted_iota(jnp.int32, s.shape, 1)
            s = jnp.where(c <= r, s, NEG_INF)
        m_prev = m_scr[rows, :]
        m_new = jnp.maximum(m_prev, jnp.max(s, -1, keepdims=True))
        p = jnp.exp2(s - jnp.concatenate([m_new] * (sub // LANES), -1))
        corr = jnp.exp2(m_prev - m_new)
        m_scr[rows, :] = m_new
        l_scr[rows, :] = corr * l_scr[rows, :] + jnp.sum(p, -1, keepdims=True)
        acc_scr[rows, :] = corr * acc_scr[rows, :] + jnp.dot(
            p.astype(MXU_DTYPE), v_ref[kv_rows(j), :], preferred_element_type=F32)

    def body(j, carry):
        k = keys(j)
        s_next = _mm_nt(q[:sub], k)
        for a in range(streams):
            s = s_next
            if a + 1 < streams:
                s_next = _mm_nt(q[(a + 1) * sub:(a + 2) * sub], k)
            softmax_pv(s, a, j, False)
        return carry

    lax.fori_loop(0, streams * blk, body, 0)
    for b in range(streams):
        j = streams * blk + b
        s_part = _mm_nt(q[b * sub:], keys(j))
        for a in range(b, streams):
            softmax_pv(s_part[(a - b) * sub:(a - b + 1) * sub], a, j, a == b)
    o_ref[...] = (acc_scr[...] / l_scr[...]).astype(o_ref.dtype)


def _flash_attention(qn, qr, kn, kr, v, B, S, H):
    T = B * S
    sub = _tile(S, 512)
    streams = _tile(S // sub, 8)
    tq = sub * streams
    nq = S // tq
    qmap = lambda b, h, i: (b * nq + i, h)
    kmap = lambda b, h, i: (b, h)
    stat = pltpu.VMEM((tq, LANES), F32)
    return pl.pallas_call(
        functools.partial(_flash_kernel, sub=sub, streams=streams),
        grid=(B, H, nq),
        in_specs=[pl.BlockSpec((tq, LANES), qmap), pl.BlockSpec((tq, LANES), qmap),
                  pl.BlockSpec((S, LANES), kmap),
                  pl.BlockSpec((S, LANES), lambda b, h, i: (b, 0)),
                  pl.BlockSpec((S, LANES), kmap)],
        out_specs=pl.BlockSpec((tq, LANES), qmap),
        out_shape=jax.ShapeDtypeStruct((T, H * MLA_V), MXU_DTYPE),
        scratch_shapes=[stat, stat, stat],
        compiler_params=_params("parallel", "parallel", "arbitrary"),
        name="mla_flash",
    )(qn, qr, kn, kr, v)


def _mla_layer(h, S, w_in, q_norm, kv_norm, w_q_b, w_kv_b, w_out, g, b, alpha):
    T, D = h.shape
    (qn, qr, kn, v, kr), H = _mla_project(h, w_in, q_norm, kv_norm, w_q_b, w_kv_b, S)
    o = _flash_attention(qn, qr, kn, kr, v, T // S, S, H)
    return _proj_res_ln(o, w_out.astype(MXU_DTYPE), jnp.zeros((D,), F32), h, g, b, alpha)


def _shift_delta(x, prev_rows, is_seq_start):
    prev = jnp.where(is_seq_start, 0.0, prev_rows[SUBLANES - 1:SUBLANES, :])
    shifted = pltpu.roll(x, 1, 0)
    first = lax.broadcasted_iota(jnp.int32, x.shape, 0) == 0
    return jnp.where(first, prev, shifted) - x


def _rwkv_rkv_kernel(x_ref, prev_ref, mix_ref, w_ref, o_ref, *, tiles_per_seq):
    i = pl.program_id(1)
    x = x_ref[...]
    xx = _shift_delta(x, prev_ref[...], i % tiles_per_seq == 0)
    xm = (x + xx * mix_ref[0]).astype(MXU_DTYPE)
    o_ref[0] = jnp.dot(xm, w_ref[0], preferred_element_type=F32)


def _rwkv_rkv(x, mix3, w3, S):
    T, D = x.shape
    tm = _tile(S, 512)
    per8 = tm // SUBLANES
    return pl.pallas_call(
        functools.partial(_rwkv_rkv_kernel, tiles_per_seq=S // tm),
        grid=(3, T // tm),
        in_specs=[pl.BlockSpec((tm, D), lambda n, i: (i, 0)),
                  pl.BlockSpec((SUBLANES, D), lambda n, i: (jnp.maximum(i * per8 - 1, 0), 0)),
                  pl.BlockSpec((1, 1, D), lambda n, i: (n, 0, 0)),
                  pl.BlockSpec((1, D, D), lambda n, i: (n, 0, 0))],
        out_specs=pl.BlockSpec((1, tm, D), lambda n, i: (n, i, 0)),
        out_shape=jax.ShapeDtypeStruct((3, T, D), F32),
        compiler_params=_params("parallel", "parallel"),
        name="rwkv_rkv",
    )(x, x, mix3, w3)


def _rwkv_lora_kernel(x_ref, prev_ref, mix_ref, w0_ref, a0_ref, w1_ref, w2_ref, a1_ref,
                      a2_ref, g1_ref, g2_ref, ld_ref, a_ref, g_ref, *, tiles_per_seq):
    i = pl.program_id(0)
    x = x_ref[...]
    xx = _shift_delta(x, prev_ref[...], i % tiles_per_seq == 0)
    xw = (x + xx * mix_ref[0:1, :]).astype(MXU_DTYPE)
    xa = (x + xx * mix_ref[1:2, :]).astype(MXU_DTYPE)
    xg = (x + xx * mix_ref[2:3, :]).astype(MXU_DTYPE)
    hw = jnp.tanh(jnp.dot(xw, w1_ref[...], preferred_element_type=F32)).astype(MXU_DTYPE)
    u = w0_ref[...] + jnp.dot(hw, w2_ref[...], preferred_element_type=F32)
    ld_ref[...] = -math.exp(-0.5) * _sigmoid(u)
    ha = jnp.dot(xa, a1_ref[...], preferred_element_type=F32).astype(MXU_DTYPE)
    a_ref[...] = _sigmoid(a0_ref[...] + jnp.dot(ha, a2_ref[...], preferred_element_type=F32))
    hg = _sigmoid(jnp.dot(xg, g1_ref[...], preferred_element_type=F32)).astype(MXU_DTYPE)
    g_ref[...] = jnp.dot(hg, g2_ref[...], preferred_element_type=F32)


def _pad_rank(w_down, w_up):
    r = w_down.shape[1]
    rp = -(-r // LANES) * LANES
    return (jnp.pad(w_down, ((0, 0), (0, rp - r))).astype(MXU_DTYPE),
            jnp.pad(w_up, ((0, rp - r), (0, 0))).astype(MXU_DTYPE))


def _rwkv_lora(x, mix3, w0, a0, w1, w2, a1, a2, g1, g2, S):
    T, D = x.shape
    tm = _tile(S, 512)
    per8 = tm // SUBLANES
    row = lambda i: (i, 0)
    out = jax.ShapeDtypeStruct((T, D), F32)
    w1p, w2p = _pad_rank(w1, w2)
    a1p, a2p = _pad_rank(a1, a2)
    g1p, g2p = _pad_rank(g1, g2)
    return pl.pallas_call(
        functools.partial(_rwkv_lora_kernel, tiles_per_seq=S // tm),
        grid=(T // tm,),
        in_specs=[pl.BlockSpec((tm, D), row),
                  pl.BlockSpec((SUBLANES, D), lambda i: (jnp.maximum(i * per8 - 1, 0), 0)),
                  _resident((3, D)), _resident((1, D)), _resident((1, D)),
                  _resident(w1p.shape), _resident(w2p.shape), _resident(a1p.shape),
                  _resident(a2p.shape), _resident(g1p.shape), _resident(g2p.shape)],
        out_specs=[pl.BlockSpec((tm, D), row)] * 3,
        out_shape=[out, out, out],
        compiler_params=_params("parallel"),
        name="rwkv_lora",
    )(x, x, mix3, w0.reshape(1, D), a0.reshape(1, D), w1p, w2p, a1p, a2p, g1p, g2p)


def _wkv_kernel(r_ref, k_ref, v_ref, ld_ref, a_ref, g_ref, kk_ref, ka_ref, rk_ref,
                lng_ref, lnb_ref, z_ref, state_ref, *, pairs, chunk, chunks):
    L = chunk

    @pl.when(pl.program_id(2) == 0)
    def _():
        state_ref[...] = jnp.zeros_like(state_ref)

    lane = lax.broadcasted_iota(jnp.int32, (1, LANES), 1)
    head0 = lane < RWKV_HEAD

    def head_sum(t):
        s0 = jnp.sum(jnp.where(head0, t, 0.0), -1, keepdims=True)
        s1 = jnp.sum(jnp.where(head0, 0.0, t), -1, keepdims=True)
        return jnp.where(head0, s0, s1)

    def stack(t):
        return jnp.concatenate([jnp.where(head0, t, 0.0), jnp.where(head0, 0.0, t)], 0)

    ld_all = ld_ref[...]
    rows_t = lax.broadcasted_iota(jnp.int32, (chunks * L, chunks * L), 0)
    cols_t = lax.broadcasted_iota(jnp.int32, (chunks * L, chunks * L), 1)
    tri = ((rows_t >= cols_t) & (rows_t // L == cols_t // L)).astype(MXU_DTYPE)
    ld_hi = ld_all.astype(MXU_DTYPE)
    rem = ld_all - ld_hi.astype(F32)
    ld_mid = rem.astype(MXU_DTYPE)
    ld_lo = (rem - ld_mid.astype(F32)).astype(MXU_DTYPE)
    cum_all = (jnp.dot(tri, ld_hi, preferred_element_type=F32)
               + jnp.dot(tri, ld_mid, preferred_element_type=F32)
               + jnp.dot(tri, ld_lo, preferred_element_type=F32))

    row_p = lax.broadcasted_iota(jnp.int32, (L, LANES), 0)
    col_p = lax.broadcasted_iota(jnp.int32, (L, LANES), 1) % RWKV_HEAD
    strict = row_p > col_p
    incl = row_p >= col_p
    eye_p = (row_p == col_p).astype(F32)
    same_head = ((lax.broadcasted_iota(jnp.int32, (LANES, LANES), 0) // RWKV_HEAD)
                 == (lax.broadcasted_iota(jnp.int32, (LANES, LANES), 1) // RWKV_HEAD))

    units = [(slice(c * L, (c + 1) * L), slice(p * LANES, (p + 1) * LANES))
             for c in range(chunks) for p in range(pairs)]
    n_units = len(units)
    lhs, rhs, v_b, v2, tails, w_chunk = [], [], [], [], [], []
    for rows, sl in units:
        r = r_ref[rows, sl]
        k = k_ref[rows, sl]
        a = a_ref[rows, sl]
        ld = ld_all[rows, sl]
        cum = cum_all[rows, sl]
        kk = k * kk_ref[:, sl]
        kk = kk / jnp.maximum(jnp.sqrt(head_sum(kk * kk)), 1e-12)
        k_mod = k * (1.0 + (a - 1.0) * ka_ref[:, sl])
        bb = kk * a
        cum_last = cum[L - 1:L, :]
        e_neg = jnp.exp(-cum)
        e_tail = jnp.exp(cum_last - cum)
        lhs.append(jnp.concatenate([-kk * jnp.exp(cum - ld), r * jnp.exp(cum)],
                                   0).astype(MXU_DTYPE))
        rhs.append(jnp.concatenate([stack(bb * e_neg), stack(k_mod * e_neg)], 0).astype(MXU_DTYPE))
        tails.append(jnp.concatenate([bb * e_tail, k_mod * e_tail], 0).astype(MXU_DTYPE))
        v = v_ref[rows, sl].astype(MXU_DTYPE)
        v_b.append(v)
        v2.append(stack(v))
        w_chunk.append(jnp.exp(cum_last))

    gram = [_mm_nt(lhs[u], rhs[u]) for u in range(n_units)]
    n_ab = [jnp.where(strict, g_[:L, :2 * L], 0.0) for g_ in gram]
    m_ak = [jnp.where(strict, g_[:L, 2 * L:], 0.0).astype(MXU_DTYPE) for g_ in gram]
    n_r = [jnp.concatenate([jnp.where(incl, g_[L:, :2 * L], 0.0),
                            jnp.where(incl, g_[L:, 2 * L:], 0.0)], 1).astype(MXU_DTYPE)
           for g_ in gram]

    levels = int(math.log2(L))
    t_inv = [eye_p + n for n in n_ab]
    n_pow = [_mm(n, stack(n)) for n in n_ab]
    for _ in range(levels - 2):
        both = [_mm(jnp.concatenate([n_pow[u], t_inv[u]], 0), stack(n_pow[u]))
                for u in range(n_units)]
        n_pow = [b_[:L] for b_ in both]
        t_inv = [t_inv[u] + both[u][L:] for u in range(n_units)]
    t_inv = [t_inv[u] + _mm(t_inv[u], stack(n_pow[u])) for u in range(n_units)]
    m_ak_v = [_mm(m_ak[u], v2[u]) for u in range(n_units)]

    state = [state_ref[p] for p in range(pairs)]
    y_all = []
    for c in range(chunks):
        us = [c * pairs + p for p in range(pairs)]
        from_state = [_mm_nt(lhs[u], state[p]) for p, u in enumerate(us)]
        x = [from_state[p][:L] + m_ak_v[u] for p, u in enumerate(us)]
        sa = [_mm(t_inv[u], stack(x[p])).astype(MXU_DTYPE) for p, u in enumerate(us)]
        y_all += [from_state[p][L:] + _mm(n_r[u], jnp.concatenate([stack(sa[p]), v2[u]], 0))
                  for p, u in enumerate(us)]
        update = [_mm_tn(jnp.concatenate([sa[p], v_b[u]], 0), tails[u])
                  for p, u in enumerate(us)]
        state = [state[p] * w_chunk[u] + jnp.where(same_head, update[p], 0.0)
                 for p, u in enumerate(us)]
    for p in range(pairs):
        state_ref[p] = state[p]

    for u, (rows, sl) in enumerate(units):
        y = y_all[u]
        r = r_ref[rows, sl]
        k_mod = k_ref[rows, sl] * (1.0 + (a_ref[rows, sl] - 1.0) * ka_ref[:, sl])
        mu = head_sum(y) * (1.0 / RWKV_HEAD)
        d = y - mu
        var = head_sum(d * d) * (1.0 / RWKV_HEAD)
        yn = d * lax.rsqrt(var + RWKV_GN_EPS) * lng_ref[:, sl] + lnb_ref[:, sl]
        bonus = head_sum(r * k_mod * rk_ref[:, sl]) * v_ref[rows, sl]
        z_ref[rows, sl] = ((yn + bonus) * g_ref[rows, sl]).astype(z_ref.dtype)


def _wkv(rkv, ld, a, g, k_k, k_a, r_k, ln_g, ln_b, B, S):
    _, T, D = rkv.shape
    L = RWKV_CHUNK
    n_pairs = D // LANES
    pairs = _tile(n_pairs, 16)
    chunks = _tile(S // L, 2)
    width = pairs * LANES
    rows = chunks * L
    nc = S // rows
    tok = lambda b, q, c: (b * nc + c, q)
    par = lambda b, q, c: (0, q)

    def rkv_spec(n):
        return pl.BlockSpec((None, rows, width), lambda b, q, c: (n, b * nc + c, q))

    vec = lambda t: t.reshape(1, D)
    return pl.pallas_call(
        functools.partial(_wkv_kernel, pairs=pairs, chunk=L, chunks=chunks),
        grid=(B, n_pairs // pairs, nc),
        in_specs=[rkv_spec(0), rkv_spec(1), rkv_spec(2),
                  pl.BlockSpec((rows, width), tok), pl.BlockSpec((rows, width), tok),
                  pl.BlockSpec((rows, width), tok)] + [pl.BlockSpec((1, width), par)] * 5,
        out_specs=pl.BlockSpec((rows, width), tok),
        out_shape=jax.ShapeDtypeStruct((T, D), MXU_DTYPE),
        scratch_shapes=[pltpu.VMEM((pairs, LANES, LANES), F32)],
        compiler_params=_params("parallel", "parallel", "arbitrary"),
        name="wkv7",
    )(rkv, rkv, rkv, ld, a, g, vec(k_k), vec(k_a), vec(r_k), vec(ln_g), vec(ln_b))


def _rwkv_layer(h, S, mix, w_in, w0, w1, w2, a0, a1, a2, g1, g2, k_k, k_a, r_k,
                ln_g, ln_b, w_out, g, b, alpha):
    T, D = h.shape
    rkv = _rwkv_rkv(h, mix[jnp.array([0, 2, 3])].reshape(3, 1, D), w_in.astype(MXU_DTYPE), S)
    ld, a, gate = _rwkv_lora(h, mix[jnp.array([1, 4, 5])], w0, a0, w1, w2, a1, a2, g1, g2, S)
    z = _wkv(rkv, ld, a, gate, k_k, k_a, r_k, ln_g, ln_b, T // S, S)
    return _proj_res_ln(z, w_out.astype(MXU_DTYPE), jnp.zeros((D,), F32), h, g, b, alpha)


def _swa_qkv_kernel(x_ref, w_ref, b_ref, q_ref, k_ref, v_ref, *, q_width, kv_width, scale):
    y = jnp.dot(x_ref[...].astype(MXU_DTYPE), w_ref[...], preferred_element_type=F32) + b_ref[...]
    q_ref[...] = (y[:, :q_width] * scale).astype(q_ref.dtype)
    k_ref[...] = y[:, q_width:q_width + kv_width].astype(k_ref.dtype)
    v_ref[...] = y[:, q_width + kv_width:].astype(v_ref.dtype)


def _swa_qkv(x, w_in, b_in, Hq, Hk):
    T, D = x.shape
    Dh = SWA_HEAD
    qw = Hq * Dh

    def dup(t):
        t3 = t[..., qw:].reshape(t.shape[:-1] + (2, Hk, Dh))
        return jnp.concatenate([t3, t3], -1).reshape(t.shape[:-1] + (2 * Hk * 2 * Dh,))

    w = jnp.concatenate([w_in[:, :qw], dup(w_in)], -1).astype(MXU_DTYPE)
    bias = jnp.concatenate([b_in[:qw], dup(b_in)], -1).reshape(1, -1)
    kvw = Hk * 2 * Dh
    tm = _tile(T, 512)
    row = lambda i: (i, 0)
    return pl.pallas_call(
        functools.partial(_swa_qkv_kernel, q_width=qw, kv_width=kvw, scale=Dh ** -0.5),
        grid=(T // tm,),
        in_specs=[pl.BlockSpec((tm, D), row), _resident(w.shape), _resident(bias.shape)],
        out_specs=[pl.BlockSpec((tm, qw), row), pl.BlockSpec((tm, kvw), row),
                   pl.BlockSpec((tm, kvw), row)],
        out_shape=[jax.ShapeDtypeStruct((T, qw), MXU_DTYPE),
                   jax.ShapeDtypeStruct((T, kvw), MXU_DTYPE),
                   jax.ShapeDtypeStruct((T, kvw), MXU_DTYPE)],
        compiler_params=_params("parallel"),
        name="swa_qkv",
    )(x, w, bias)


def _swa_bias_kernel(bucket_ref, window_ref, rel_ref, sinks_ref, o_ref, sink_rows_ref, *, buckets):
    h = pl.program_id(0)
    bucket = bucket_ref[...]
    acc = jnp.zeros(bucket.shape, F32)
    for n in range(buckets):
        acc = jnp.where(bucket == n, rel_ref[n, h], acc)
    o_ref[0] = jnp.where(window_ref[...] > 0, acc, NEG_INF)
    sink_rows_ref[0] = jnp.full(sink_rows_ref.shape[1:], sinks_ref[h], F32)


def _t5_bucket(dist):
    max_exact = REL_BUCKETS // 2
    n = jnp.maximum(dist, 0)
    nf = jnp.maximum(n, 1).astype(F32)
    large = max_exact + (jnp.log(nf / max_exact) / math.log(SWA_WINDOW / max_exact)
                         * (REL_BUCKETS - max_exact)).astype(jnp.int32)
    large = jnp.minimum(large, REL_BUCKETS - 1)
    return jnp.where(n < max_exact, n, large)


def _swa_bias(rel_bias, sinks):
    W = SWA_WINDOW
    Hq = rel_bias.shape[1]
    dist = jnp.arange(W)[:, None] + W - jnp.arange(2 * W)[None, :]
    bucket = _t5_bucket(dist).astype(jnp.int32)
    in_window = ((dist >= 0) & (dist < W)).astype(jnp.int32)
    return pl.pallas_call(
        functools.partial(_swa_bias_kernel, buckets=REL_BUCKETS),
        grid=(Hq,),
        in_specs=[_resident((W, 2 * W)), _resident((W, 2 * W)),
                  pl.BlockSpec(memory_space=pltpu.SMEM), pl.BlockSpec(memory_space=pltpu.SMEM)],
        out_specs=[pl.BlockSpec((1, W, 2 * W), lambda h: (h, 0, 0)),
                   pl.BlockSpec((1, W, LANES), lambda h: (h, 0, 0))],
        out_shape=[jax.ShapeDtypeStruct((Hq, W, 2 * W), F32),
                   jax.ShapeDtypeStruct((Hq, W, LANES), F32)],
        compiler_params=_params("arbitrary"),
        name="swa_bias",
    )(bucket, in_window, rel_bias, sinks)


def _swa_attn_kernel(q_ref, kc_ref, kp_ref, vc_ref, vp_ref, bias_ref, sink_ref, o_ref,
                     *, kv_heads, group):
    W = SWA_WINDOW
    i = pl.program_id(1)
    lane = lax.broadcasted_iota(jnp.int32, (1, LANES), 1)
    half0 = lane < SWA_HEAD
    rows = group * W
    col = lax.broadcasted_iota(jnp.int32, (rows, 2 * W), 1)
    key_ok = (col >= W) | (i > 0)
    scores = []
    for kv in range(kv_heads):
        ksl = slice(kv * LANES, (kv + 1) * LANES)
        k_band = jnp.concatenate([kp_ref[:, ksl], kc_ref[:, ksl]], 0)
        parts = []
        for pair in range(group // 2):
            blk = kv * (group // 2) + pair
            qp = q_ref[:, blk * LANES:(blk + 1) * LANES]
            zero = jnp.zeros_like(qp)
            parts += [jnp.where(half0, qp, zero), jnp.where(half0, zero, qp)]
        scores.append(_mm_nt(jnp.concatenate(parts, 0), k_band))
    for kv in range(kv_heads):
        ksl = slice(kv * LANES, (kv + 1) * LANES)
        v_band = jnp.concatenate([vp_ref[:, ksl], vc_ref[:, ksl]], 0)
        heads = slice(kv * group, (kv + 1) * group)
        s = scores[kv] + bias_ref[heads].reshape(rows, 2 * W)
        s = jnp.where(key_ok, s, NEG_INF)
        sink = sink_ref[heads].reshape(rows, LANES)
        m = jnp.maximum(jnp.max(s, -1, keepdims=True), sink)
        p = jnp.exp(s - jnp.concatenate([m] * (2 * W // LANES), -1))
        denom = jnp.sum(p, -1, keepdims=True) + jnp.exp(sink - m)
        o = jnp.dot(p.astype(MXU_DTYPE), v_band, preferred_element_type=F32) / denom
        for pair in range(group // 2):
            blk = kv * (group // 2) + pair
            o0 = o[(2 * pair) * W:(2 * pair + 1) * W]
            o1 = o[(2 * pair + 1) * W:(2 * pair + 2) * W]
            o_ref[:, blk * LANES:(blk + 1) * LANES] = jnp.where(half0, o0, o1).astype(o_ref.dtype)


def _swa_attention(q, kd, vd, bias, sink_rows, B, S, Hq, Hk):
    T = B * S
    W = SWA_WINDOW
    nb = S // W
    qw = Hq * SWA_HEAD
    kvw = Hk * LANES
    cur = lambda b, i: (b * nb + i, 0)
    prev = lambda b, i: (b * nb + jnp.maximum(i - 1, 0), 0)
    return pl.pallas_call(
        functools.partial(_swa_attn_kernel, kv_heads=Hk, group=Hq // Hk),
        grid=(B, nb),
        in_specs=[pl.BlockSpec((W, qw), cur),
                  pl.BlockSpec((W, kvw), cur), pl.BlockSpec((W, kvw), prev),
                  pl.BlockSpec((W, kvw), cur), pl.BlockSpec((W, kvw), prev),
                  _resident(bias.shape), _resident(sink_rows.shape)],
        out_specs=pl.BlockSpec((W, qw), cur),
        out_shape=jax.ShapeDtypeStruct((T, qw), MXU_DTYPE),
        compiler_params=_params("parallel", "arbitrary"),
        name="swa_attn",
    )(q, kd, kd, vd, vd, bias, sink_rows)


def _swa_layer(h, S, w_in, b_in, sinks, w_out, b_out, rel_bias, g, b, alpha):
    T, D = h.shape
    Hq = D // SWA_HEAD
    Hk = (w_in.shape[1] - D) // (2 * SWA_HEAD)
    assert (Hq // Hk) % 2 == 0
    q, kd, vd = _swa_qkv(h, w_in, b_in, Hq, Hk)
    bias, sink_rows = _swa_bias(rel_bias, sinks)
    o = _swa_attention(q, kd, vd, bias, sink_rows, T // S, S, Hq, Hk)
    return _proj_res_ln(o, w_out.astype(MXU_DTYPE), b_out, h, g, b, alpha)


def kernel(x, mla_w_in, mla_q_norm, mla_kv_norm, mla_w_q_b, mla_w_kv_b, mla_w_out, rwkv_mix, rwkv_w_in, rwkv_w0, rwkv_w1, rwkv_w2, rwkv_a0, rwkv_a1, rwkv_a2, rwkv_g1, rwkv_g2, rwkv_k_k, rwkv_k_a, rwkv_r_k, rwkv_ln_g, rwkv_ln_b, rwkv_w_out, swa_w_in, swa_b_in, swa_sinks, swa_w_out, swa_b_out, rel_bias, ln_g, ln_b, mlp_up, mlp_down):
    B, S, D = x.shape
    depth = ln_g.shape[0]
    alpha = (2.0 * depth) ** 0.25
    h = x.reshape(B * S, D)
    mlp_up_b = mlp_up.astype(MXU_DTYPE)
    mlp_down_b = mlp_down.astype(MXU_DTYPE)
    for i in range(depth):
        j = i // 3
        kind = i % 3
        g0, b0 = ln_g[i, 0], ln_b[i, 0]
        if kind == 0:
            h = _mla_layer(h, S, mla_w_in[j], mla_q_norm[j], mla_kv_norm[j], mla_w_q_b[j],
                           mla_w_kv_b[j], mla_w_out[j], g0, b0, alpha)
        elif kind == 1:
            h = _rwkv_layer(h, S, rwkv_mix[j], rwkv_w_in[j], rwkv_w0[j], rwkv_w1[j], rwkv_w2[j],
                            rwkv_a0[j], rwkv_a1[j], rwkv_a2[j], rwkv_g1[j], rwkv_g2[j],
                            rwkv_k_k[j], rwkv_k_a[j], rwkv_r_k[j].reshape(-1), rwkv_ln_g[j],
                            rwkv_ln_b[j], rwkv_w_out[j], g0, b0, alpha)
        else:
            h = _swa_layer(h, S, swa_w_in[j], swa_b_in[j], swa_sinks[j], swa_w_out[j],
                           swa_b_out[j], rel_bias, g0, b0, alpha)
        h = _mlp_res_ln(h, mlp_up_b, mlp_down_b, i, ln_g[i, 1], ln_b[i, 1], alpha)
    return h.reshape(B, S, D)
```

```python
import functools
import math

import jax
import jax.numpy as jnp
from jax import lax
from jax.experimental import pallas as pl
from jax.experimental.pallas import tpu as pltpu

F32 = jnp.float32
MXU_DTYPE = jnp.bfloat16

LN_EPS = 1e-5
RMS_EPS = 1e-6
NEG_INF = -1e30
ROPE_THETA = 10000.0

LANES = 128
SUBLANES = 8
VMEM_LIMIT = 56 * 1024 * 1024

MLA_NOPE = 128
MLA_ROPE = 64
MLA_V = 128
RWKV_HEAD = 64
RWKV_GN_EPS = 64e-5
RWKV_CHUNK = 64
SWA_HEAD = 64
SWA_WINDOW = 128
REL_BUCKETS = 32


def _params(*sem):
    return pltpu.CompilerParams(dimension_semantics=sem, vmem_limit_bytes=VMEM_LIMIT)


def _tile(n, want):
    t = min(n, want)
    assert n % t == 0, (n, t)
    return t


def _resident(shape):
    nd = len(shape)
    return pl.BlockSpec(shape, lambda *_: (0,) * nd, pipeline_mode=pl.Buffered(1))


def _mm(a, b):
    return jnp.dot(a.astype(MXU_DTYPE), b.astype(MXU_DTYPE), preferred_element_type=F32)


def _mm_nt(a, b):
    return lax.dot_general(a.astype(MXU_DTYPE), b.astype(MXU_DTYPE),
                           (((1,), (1,)), ((), ())), preferred_element_type=F32)


def _mm_tn(a, b):
    return lax.dot_general(a.astype(MXU_DTYPE), b.astype(MXU_DTYPE),
                           (((0,), (0,)), ((), ())), preferred_element_type=F32)


def _split(a):
    hi = a.astype(MXU_DTYPE)
    lo = (a - hi.astype(F32)).astype(MXU_DTYPE)
    return hi, lo


def _mm3(a, b):
    ah, al = _split(a)
    bh, bl = _split(b)
    return (jnp.dot(ah, bh, preferred_element_type=F32)
            + jnp.dot(ah, bl, preferred_element_type=F32)
            + jnp.dot(al, bh, preferred_element_type=F32))


def _layer_norm(y, g, b):
    mu = jnp.mean(y, -1, keepdims=True)
    d = y - mu
    var = jnp.mean(d * d, -1, keepdims=True)
    return d * lax.rsqrt(var + LN_EPS) * g + b


def _rms_norm(y, g):
    return y * lax.rsqrt(jnp.mean(y * y, -1, keepdims=True) + RMS_EPS) * g


def _sigmoid(z):
    return 1.0 / (1.0 + jnp.exp(-z))


def _proj_res_ln_kernel(a_ref, w_ref, bias_ref, x_ref, g_ref, b_ref, o_ref, *, alpha):
    y = jnp.dot(a_ref[...], w_ref[...], preferred_element_type=F32)
    y = y + bias_ref[...] + alpha * x_ref[...]
    o_ref[...] = _layer_norm(y, g_ref[...], b_ref[...])


def _proj_res_ln(a, w, bias, x, g, b, alpha):
    T, K = a.shape
    D = w.shape[1]
    tm = _tile(T, 512)
    return pl.pallas_call(
        functools.partial(_proj_res_ln_kernel, alpha=alpha),
        grid=(T // tm,),
        in_specs=[pl.BlockSpec((tm, K), lambda i: (i, 0)),
                  _resident((K, D)), _resident((1, D)),
                  pl.BlockSpec((tm, D), lambda i: (i, 0)),
                  _resident((1, D)), _resident((1, D))],
        out_specs=pl.BlockSpec((tm, D), lambda i: (i, 0)),
        out_shape=jax.ShapeDtypeStruct((T, D), F32),
        compiler_params=_params("parallel"),
        name="proj_res_ln",
    )(a, w, bias.reshape(1, D), x, g.reshape(1, D), b.reshape(1, D))


def _mlp_kernel(x_ref, wu_ref, wd_ref, g_ref, b_ref, o_ref, xb_ref, *, alpha, inner):
    j = pl.program_id(1)
    width = wu_ref.shape[1] // inner

    def chunk(xb, n):
        cols = slice(n * width, (n + 1) * width)
        u = jnp.maximum(jnp.dot(xb, wu_ref[:, cols], preferred_element_type=F32), 0.0)
        u = (u * u).astype(MXU_DTYPE)
        return jnp.dot(u, wd_ref[cols, :], preferred_element_type=F32)

    @pl.when(j == 0)
    def _():
        x = x_ref[...]
        xb = x.astype(MXU_DTYPE)
        xb_ref[...] = xb
        o_ref[...] = alpha * x + chunk(xb, 0)
        for n in range(1, inner):
            o_ref[...] += chunk(xb_ref[...], n)

    @pl.when(j > 0)
    def _():
        for n in range(inner):
            o_ref[...] += chunk(xb_ref[...], n)

    @pl.when(j == pl.num_programs(1) - 1)
    def _():
        o_ref[...] = _layer_norm(o_ref[...], g_ref[...], b_ref[...])


def _mlp_res_ln(x, wu_all, wd_all, layer, g, b, alpha):
    T, D = x.shape
    Hd = wu_all.shape[2]
    tm = _tile(T, 512)
    th = _tile(Hd, 2048)
    inner = th // _tile(th, 1024)
    return pl.pallas_call(
        functools.partial(_mlp_kernel, alpha=alpha, inner=inner),
        grid=(T // tm, Hd // th),
        in_specs=[pl.BlockSpec((tm, D), lambda i, j: (i, 0)),
                  pl.BlockSpec((None, D, th), lambda i, j: (layer, 0, j)),
                  pl.BlockSpec((None, th, D), lambda i, j: (layer, j, 0)),
                  _resident((1, D)), _resident((1, D))],
        out_specs=pl.BlockSpec((tm, D), lambda i, j: (i, 0)),
        out_shape=jax.ShapeDtypeStruct((T, D), F32),
        scratch_shapes=[pltpu.VMEM((tm, D), MXU_DTYPE)],
        compiler_params=_params("parallel", "arbitrary"),
        name="mlp_res_ln",
    )(x, wu_all, wd_all, g.reshape(1, D), b.reshape(1, D))


def _mla_proj_kernel(x_ref, wqa_ref, wkva_ref, wkr_ref, wkrr_ref, qg_ref, kvg_ref,
                     wqn_ref, wqr_ref, wqrr_ref, wkn_ref, wv_ref, cos_ref, sin_ref,
                     qn_ref, qr_ref, kn_ref, v_ref, kr_ref, *, heads, scale):
    xb = x_ref[...].astype(MXU_DTYPE)
    cos = cos_ref[...]
    sin = sin_ref[...]
    q_lat = jnp.dot(xb, wqa_ref[...], preferred_element_type=F32)
    kv_lat = jnp.dot(xb, wkva_ref[...], preferred_element_type=F32)
    kr = jnp.dot(xb, wkr_ref[...], preferred_element_type=F32)
    krr = jnp.dot(xb, wkrr_ref[...], preferred_element_type=F32)
    kr_ref[...] = (kr * cos + krr * sin).astype(kr_ref.dtype)

    qn = _rms_norm(q_lat, qg_ref[...]).astype(MXU_DTYPE)
    qn_ref[...] = (jnp.dot(qn, wqn_ref[...], preferred_element_type=F32) * scale).astype(qn_ref.dtype)
    qr = jnp.dot(qn, wqr_ref[...], preferred_element_type=F32)
    qrr = jnp.dot(qn, wqrr_ref[...], preferred_element_type=F32)
    for h in range(heads):
        sl = slice(h * LANES, (h + 1) * LANES)
        qr_ref[:, sl] = ((qr[:, sl] * cos + qrr[:, sl] * sin) * scale).astype(qr_ref.dtype)

    kvn = _rms_norm(kv_lat, kvg_ref[...]).astype(MXU_DTYPE)
    kn_ref[...] = jnp.dot(kvn, wkn_ref[...], preferred_element_type=F32).astype(kn_ref.dtype)
    v_ref[...] = jnp.dot(kvn, wv_ref[...], preferred_element_type=F32).astype(v_ref.dtype)


def _rot_half_cols(w):
    half = w.shape[-1] // 2
    return jnp.concatenate([-w[..., half:], w[..., :half]], -1)


def _pad_lanes(w, width):
    pad = [(0, 0)] * (w.ndim - 1) + [(0, width - w.shape[-1])]
    return jnp.pad(w, pad)


def _mla_project(x, w_in, q_norm, kv_norm, w_q_b, w_kv_b, S):
    T, D = x.shape
    q_rank = q_norm.shape[-1]
    kv_rank = kv_norm.shape[-1]
    H = w_q_b.shape[1] // (MLA_NOPE + MLA_ROPE)
    scale = (MLA_NOPE + MLA_ROPE) ** -0.5 * math.log2(math.e)

    wqa = w_in[:, :q_rank].astype(MXU_DTYPE)
    wkva = w_in[:, q_rank:q_rank + kv_rank].astype(MXU_DTYPE)
    wkr_raw = w_in[:, q_rank + kv_rank:]
    wkr = _pad_lanes(wkr_raw, LANES).astype(MXU_DTYPE)
    wkrr = _pad_lanes(_rot_half_cols(wkr_raw), LANES).astype(MXU_DTYPE)
    wq3 = w_q_b.reshape(q_rank, H, MLA_NOPE + MLA_ROPE)
    wqn = wq3[:, :, :MLA_NOPE].reshape(q_rank, H * MLA_NOPE).astype(MXU_DTYPE)
    wq_rope = wq3[:, :, MLA_NOPE:]
    wqr = _pad_lanes(wq_rope, LANES).reshape(q_rank, H * LANES).astype(MXU_DTYPE)
    wqrr = _pad_lanes(_rot_half_cols(wq_rope), LANES).reshape(q_rank, H * LANES).astype(MXU_DTYPE)
    wkv3 = w_kv_b.reshape(kv_rank, H, MLA_NOPE + MLA_V)
    wkn = wkv3[:, :, :MLA_NOPE].reshape(kv_rank, H * MLA_NOPE).astype(MXU_DTYPE)
    wv = wkv3[:, :, MLA_NOPE:].reshape(kv_rank, H * MLA_V).astype(MXU_DTYPE)

    half = MLA_ROPE // 2
    inv = ROPE_THETA ** (-jnp.arange(half, dtype=F32) / half)
    ang = jnp.arange(S, dtype=F32)[:, None] * inv[None, :]
    cos = _pad_lanes(jnp.concatenate([jnp.cos(ang), jnp.cos(ang)], -1), LANES)
    sin = _pad_lanes(jnp.concatenate([jnp.sin(ang), jnp.sin(ang)], -1), LANES)

    tm = _tile(S, 512)
    ns = S // tm
    row = lambda i: (i, 0)
    pos = lambda i: (i % ns, 0)
    wide = jax.ShapeDtypeStruct((T, H * LANES), MXU_DTYPE)
    outs = pl.pallas_call(
        functools.partial(_mla_proj_kernel, heads=H, scale=scale),
        grid=(T // tm,),
        in_specs=[pl.BlockSpec((tm, D), row),
                  _resident(wqa.shape), _resident(wkva.shape), _resident(wkr.shape),
                  _resident(wkrr.shape), _resident((1, q_rank)), _resident((1, kv_rank)),
                  _resident(wqn.shape), _resident(wqr.shape), _resident(wqrr.shape),
                  _resident(wkn.shape), _resident(wv.shape),
                  pl.BlockSpec((tm, LANES), pos), pl.BlockSpec((tm, LANES), pos)],
        out_specs=[pl.BlockSpec((tm, H * LANES), row)] * 4 + [pl.BlockSpec((tm, LANES), row)],
        out_shape=[wide, wide, wide, wide, jax.ShapeDtypeStruct((T, LANES), MXU_DTYPE)],
        compiler_params=_params("parallel"),
        name="mla_proj",
    )(x, wqa, wkva, wkr, wkrr, q_norm.reshape(1, -1), kv_norm.reshape(1, -1),
      wqn, wqr, wqrr, wkn, wv, cos, sin)
    return outs, H


def _flash_kernel(qn_ref, qr_ref, kn_ref, kr_ref, v_ref, o_ref, m_scr, l_scr, acc_scr,
                  *, sub, streams):
    blk = pl.program_id(2)
    q = jnp.concatenate([qn_ref[...], qr_ref[...]], -1)
    m_scr[...] = jnp.full(m_scr.shape, NEG_INF, F32)
    l_scr[...] = jnp.zeros(l_scr.shape, F32)
    acc_scr[...] = jnp.zeros(acc_scr.shape, F32)

    def kv_rows(j):
        return pl.ds(pl.multiple_of(j * sub, sub), sub)

    def keys(j):
        return jnp.concatenate([kn_ref[kv_rows(j), :], kr_ref[kv_rows(j), :]], -1)

    def softmax_pv(s, a, j, diagonal):
        rows = slice(a * sub, (a + 1) * sub)
        if diagonal:
            r = lax.broadcasted_iota(jnp.int32, s.shape, 0)
            c = lax.broadcasted_iota(jnp.int32, s.shape, 1)
            s = jnp.where(c <= r, s, NEG_INF)
        m_prev = m_scr[rows, :]
        m_new = jnp.maximum(m_prev, jnp.max(s, -1, keepdims=True))
        p = jnp.exp2(s - jnp.concatenate([m_new] * (sub // LANES), -1))
        corr = jnp.exp2(m_prev - m_new)
        m_scr[rows, :] = m_new
        l_scr[rows, :] = corr * l_scr[rows, :] + jnp.sum(p, -1, keepdims=True)
        acc_scr[rows, :] = corr * acc_scr[rows, :] + jnp.dot(
            p.astype(MXU_DTYPE), v_ref[kv_rows(j), :], preferred_element_type=F32)

    def body(j, carry):
        k = keys(j)
        s_next = _mm_nt(q[:sub], k)
        for a in range(streams):
            s = s_next
            if a + 1 < streams:
                s_next = _mm_nt(q[(a + 1) * sub:(a + 2) * sub], k)
            softmax_pv(s, a, j, False)
        return carry

    lax.fori_loop(0, streams * blk, body, 0)
    for b in range(streams):
        j = streams * blk + b
        s_part = _mm_nt(q[b * sub:], keys(j))
        for a in range(b, streams):
            softmax_pv(s_part[(a - b) * sub:(a - b + 1) * sub], a, j, a == b)
    o_ref[...] = (acc_scr[...] / l_scr[...]).astype(o_ref.dtype)


def _flash_attention(qn, qr, kn, kr, v, B, S, H):
    T = B * S
    sub = _tile(S, 512)
    streams = _tile(S // sub, 8)
    tq = sub * streams
    nq = S // tq
    qmap = lambda b, h, i: (b * nq + i, h)
    kmap = lambda b, h, i: (b, h)
    stat = pltpu.VMEM((tq, LANES), F32)
    return pl.pallas_call(
        functools.partial(_flash_kernel, sub=sub, streams=streams),
        grid=(B, H, nq),
        in_specs=[pl.BlockSpec((tq, LANES), qmap), pl.BlockSpec((tq, LANES), qmap),
                  pl.BlockSpec((S, LANES), kmap),
                  pl.BlockSpec((S, LANES), lambda b, h, i: (b, 0)),
                  pl.BlockSpec((S, LANES), kmap)],
        out_specs=pl.BlockSpec((tq, LANES), qmap),
        out_shape=jax.ShapeDtypeStruct((T, H * MLA_V), MXU_DTYPE),
        scratch_shapes=[stat, stat, stat],
        compiler_params=_params("parallel", "parallel", "arbitrary"),
        name="mla_flash",
    )(qn, qr, kn, kr, v)


def _mla_layer(h, S, w_in, q_norm, kv_norm, w_q_b, w_kv_b, w_out, g, b, alpha):
    T, D = h.shape
    (qn, qr, kn, v, kr), H = _mla_project(h, w_in, q_norm, kv_norm, w_q_b, w_kv_b, S)
    o = _flash_attention(qn, qr, kn, kr, v, T // S, S, H)
    return _proj_res_ln(o, w_out.astype(MXU_DTYPE), jnp.zeros((D,), F32), h, g, b, alpha)


def _shift_delta(x, prev_rows, is_seq_start):
    prev = jnp.where(is_seq_start, 0.0, prev_rows[SUBLANES - 1:SUBLANES, :])
    shifted = pltpu.roll(x, 1, 0)
    first = lax.broadcasted_iota(jnp.int32, x.shape, 0) == 0
    return jnp.where(first, prev, shifted) - x


def _rwkv_rkv_kernel(x_ref, prev_ref, mix_ref, w_ref, o_ref, *, tiles_per_seq):
    i = pl.program_id(1)
    x = x_ref[...]
    xx = _shift_delta(x, prev_ref[...], i % tiles_per_seq == 0)
    xm = (x + xx * mix_ref[0]).astype(MXU_DTYPE)
    o_ref[0] = jnp.dot(xm, w_ref[0], preferred_element_type=F32)


def _rwkv_rkv(x, mix3, w3, S):
    T, D = x.shape
    tm = _tile(S, 512)
    per8 = tm // SUBLANES
    return pl.pallas_call(
        functools.partial(_rwkv_rkv_kernel, tiles_per_seq=S // tm),
        grid=(3, T // tm),
        in_specs=[pl.BlockSpec((tm, D), lambda n, i: (i, 0)),
                  pl.BlockSpec((SUBLANES, D), lambda n, i: (jnp.maximum(i * per8 - 1, 0), 0)),
                  pl.BlockSpec((1, 1, D), lambda n, i: (n, 0, 0)),
                  pl.BlockSpec((1, D, D), lambda n, i: (n, 0, 0))],
        out_specs=pl.BlockSpec((1, tm, D), lambda n, i: (n, i, 0)),
        out_shape=jax.ShapeDtypeStruct((3, T, D), F32),
        compiler_params=_params("parallel", "parallel"),
        name="rwkv_rkv",
    )(x, x, mix3, w3)


def _rwkv_lora_kernel(x_ref, prev_ref, mix_ref, w0_ref, a0_ref, w1_ref, w2_ref, a1_ref,
                      a2_ref, g1_ref, g2_ref, ld_ref, a_ref, g_ref, *, tiles_per_seq):
    i = pl.program_id(0)
    x = x_ref[...]
    xx = _shift_delta(x, prev_ref[...], i % tiles_per_seq == 0)
    xw = (x + xx * mix_ref[0:1, :]).astype(MXU_DTYPE)
    xa = (x + xx * mix_ref[1:2, :]).astype(MXU_DTYPE)
    xg = (x + xx * mix_ref[2:3, :]).astype(MXU_DTYPE)
    hw = jnp.tanh(jnp.dot(xw, w1_ref[...], preferred_element_type=F32)).astype(MXU_DTYPE)
    u = w0_ref[...] + jnp.dot(hw, w2_ref[...], preferred_element_type=F32)
    ld_ref[...] = -math.exp(-0.5) * _sigmoid(u)
    ha = jnp.dot(xa, a1_ref[...], preferred_element_type=F32).astype(MXU_DTYPE)
    a_ref[...] = _sigmoid(a0_ref[...] + jnp.dot(ha, a2_ref[...], preferred_element_type=F32))
    hg = _sigmoid(jnp.dot(xg, g1_ref[...], preferred_element_type=F32)).astype(MXU_DTYPE)
    g_ref[...] = jnp.dot(hg, g2_ref[...], preferred_element_type=F32)


def _pad_rank(w_down, w_up):
    r = w_down.shape[1]
    rp = -(-r // LANES) * LANES
    return (jnp.pad(w_down, ((0, 0), (0, rp - r))).astype(MXU_DTYPE),
            jnp.pad(w_up, ((0, rp - r), (0, 0))).astype(MXU_DTYPE))


def _rwkv_lora(x, mix3, w0, a0, w1, w2, a1, a2, g1, g2, S):
    T, D = x.shape
    tm = _tile(S, 512)
    per8 = tm // SUBLANES
    row = lambda i: (i, 0)
    out = jax.ShapeDtypeStruct((T, D), F32)
    w1p, w2p = _pad_rank(w1, w2)
    a1p, a2p = _pad_rank(a1, a2)
    g1p, g2p = _pad_rank(g1, g2)
    return pl.pallas_call(
        functools.partial(_rwkv_lora_kernel, tiles_per_seq=S // tm),
        grid=(T // tm,),
        in_specs=[pl.BlockSpec((tm, D), row),
                  pl.BlockSpec((SUBLANES, D), lambda i: (jnp.maximum(i * per8 - 1, 0), 0)),
                  _resident((3, D)), _resident((1, D)), _resident((1, D)),
                  _resident(w1p.shape), _resident(w2p.shape), _resident(a1p.shape),
                  _resident(a2p.shape), _resident(g1p.shape), _resident(g2p.shape)],
        out_specs=[pl.BlockSpec((tm, D), row)] * 3,
        out_shape=[out, out, out],
        compiler_params=_params("parallel"),
        name="rwkv_lora",
    )(x, x, mix3, w0.reshape(1, D), a0.reshape(1, D), w1p, w2p, a1p, a2p, g1p, g2p)


def _wkv_kernel(r_ref, k_ref, v_ref, ld_ref, a_ref, g_ref, kk_ref, ka_ref, rk_ref,
                lng_ref, lnb_ref, z_ref, state_ref, *, pairs, chunk, chunks):
    L = chunk

    @pl.when(pl.program_id(2) == 0)
    def _():
        state_ref[...] = jnp.zeros_like(state_ref)

    lane = lax.broadcasted_iota(jnp.int32, (1, LANES), 1)
    head0 = lane < RWKV_HEAD

    def head_sum(t):
        s0 = jnp.sum(jnp.where(head0, t, 0.0), -1, keepdims=True)
        s1 = jnp.sum(jnp.where(head0, 0.0, t), -1, keepdims=True)
        return jnp.where(head0, s0, s1)

    def stack(t):
        return jnp.concatenate([jnp.where(head0, t, 0.0), jnp.where(head0, 0.0, t)], 0)

    ld_all = ld_ref[...]
    rows_t = lax.broadcasted_iota(jnp.int32, (chunks * L, chunks * L), 0)
    cols_t = lax.broadcasted_iota(jnp.int32, (chunks * L, chunks * L), 1)
    tri = ((rows_t >= cols_t) & (rows_t // L == cols_t // L)).astype(MXU_DTYPE)
    ld_hi = ld_all.astype(MXU_DTYPE)
    rem = ld_all - ld_hi.astype(F32)
    ld_mid = rem.astype(MXU_DTYPE)
    ld_lo = (rem - ld_mid.astype(F32)).astype(MXU_DTYPE)
    cum_all = (jnp.dot(tri, ld_hi, preferred_element_type=F32)
               + jnp.dot(tri, ld_mid, preferred_element_type=F32)
               + jnp.dot(tri, ld_lo, preferred_element_type=F32))

    row_p = lax.broadcasted_iota(jnp.int32, (L, LANES), 0)
    col_p = lax.broadcasted_iota(jnp.int32, (L, LANES), 1) % RWKV_HEAD
    strict = row_p > col_p
    incl = row_p >= col_p
    eye_p = (row_p == col_p).astype(F32)
    same_head = ((lax.broadcasted_iota(jnp.int32, (LANES, LANES), 0) // RWKV_HEAD)
                 == (lax.broadcasted_iota(jnp.int32, (LANES, LANES), 1) // RWKV_HEAD))

    units = [(slice(c * L, (c + 1) * L), slice(p * LANES, (p + 1) * LANES))
             for c in range(chunks) for p in range(pairs)]
    n_units = len(units)
    lhs, rhs, v_b, v2, tails, w_chunk = [], [], [], [], [], []
    for rows, sl in units:
        r = r_ref[rows, sl]
        k = k_ref[rows, sl]
        a = a_ref[rows, sl]
        ld = ld_all[rows, sl]
        cum = cum_all[rows, sl]
        kk = k * kk_ref[:, sl]
        kk = kk / jnp.maximum(jnp.sqrt(head_sum(kk * kk)), 1e-12)
        k_mod = k * (1.0 + (a - 1.0) * ka_ref[:, sl])
        bb = kk * a
        cum_last = cum[L - 1:L, :]
        e_neg = jnp.exp(-cum)
        e_tail = jnp.exp(cum_last - cum)
        lhs.append(jnp.concatenate([-kk * jnp.exp(cum - ld), r * jnp.exp(cum)],
                                   0).astype(MXU_DTYPE))
        rhs.append(jnp.concatenate([stack(bb * e_neg), stack(k_mod * e_neg)], 0).astype(MXU_DTYPE))
        tails.append(jnp.concatenate([bb * e_tail, k_mod * e_tail], 0).astype(MXU_DTYPE))
        v = v_ref[rows, sl].astype(MXU_DTYPE)
        v_b.append(v)
        v2.append(stack(v))
        w_chunk.append(jnp.exp(cum_last))

    gram = [_mm_nt(lhs[u], rhs[u]) for u in range(n_units)]
    n_ab = [jnp.where(strict, g_[:L, :2 * L], 0.0) for g_ in gram]
    m_ak = [jnp.where(strict, g_[:L, 2 * L:], 0.0).astype(MXU_DTYPE) for g_ in gram]
    n_r = [jnp.concatenate([jnp.where(incl, g_[L:, :2 * L], 0.0),
                            jnp.where(incl, g_[L:, 2 * L:], 0.0)], 1).astype(MXU_DTYPE)
           for g_ in gram]

    levels = int(math.log2(L))
    t_inv = [eye_p + n for n in n_ab]
    n_pow = [_mm(n, stack(n)) for n in n_ab]
    for _ in range(levels - 2):
        both = [_mm(jnp.concatenate([n_pow[u], t_inv[u]], 0), stack(n_pow[u]))
                for u in range(n_units)]
        n_pow = [b_[:L] for b_ in both]
        t_inv = [t_inv[u] + both[u][L:] for u in range(n_units)]
    t_inv = [t_inv[u] + _mm(t_inv[u], stack(n_pow[u])) for u in range(n_units)]
    m_ak_v = [_mm(m_ak[u], v2[u]) for u in range(n_units)]

    state = [state_ref[p] for p in range(pairs)]
    y_all = []
    for c in range(chunks):
        us = [c * pairs + p for p in range(pairs)]
        from_state = [_mm_nt(lhs[u], state[p]) for p, u in enumerate(us)]
        x = [from_state[p][:L] + m_ak_v[u] for p, u in enumerate(us)]
        sa = [_mm(t_inv[u], stack(x[p])).astype(MXU_DTYPE) for p, u in enumerate(us)]
        y_all += [from_state[p][L:] + _mm(n_r[u], jnp.concatenate([stack(sa[p]), v2[u]], 0))
                  for p, u in enumerate(us)]
        update = [_mm_tn(jnp.concatenate([sa[p], v_b[u]], 0), tails[u])
                  for p, u in enumerate(us)]
        state = [state[p] * w_chunk[u] + jnp.where(same_head, update[p], 0.0)
                 for p, u in enumerate(us)]
    for p in range(pairs):
        state_ref[p] = state[p]

    for u, (rows, sl) in enumerate(units):
        y = y_all[u]
        r = r_ref[rows, sl]
        k_mod = k_ref[rows, sl] * (1.0 + (a_ref[rows, sl] - 1.0) * ka_ref[:, sl])
        mu = head_sum(y) * (1.0 / RWKV_HEAD)
        d = y - mu
        var = head_sum(d * d) * (1.0 / RWKV_HEAD)
        yn = d * lax.rsqrt(var + RWKV_GN_EPS) * lng_ref[:, sl] + lnb_ref[:, sl]
        bonus = head_sum(r * k_mod * rk_ref[:, sl]) * v_ref[rows, sl]
        z_ref[rows, sl] = ((yn + bonus) * g_ref[rows, sl]).astype(z_ref.dtype)


def _wkv(rkv, ld, a, g, k_k, k_a, r_k, ln_g, ln_b, B, S):
    _, T, D = rkv.shape
    L = RWKV_CHUNK
    n_pairs = D // LANES
    pairs = _tile(n_pairs, 16)
    chunks = _tile(S // L, 2)
    width = pairs * LANES
    rows = chunks * L
    nc = S // rows
    tok = lambda b, q, c: (b * nc + c, q)
    par = lambda b, q, c: (0, q)

    def rkv_spec(n):
        return pl.BlockSpec((None, rows, width), lambda b, q, c: (n, b * nc + c, q))

    vec = lambda t: t.reshape(1, D)
    return pl.pallas_call(
        functools.partial(_wkv_kernel, pairs=pairs, chunk=L, chunks=chunks),
        grid=(B, n_pairs // pairs, nc),
        in_specs=[rkv_spec(0), rkv_spec(1), rkv_spec(2),
                  pl.BlockSpec((rows, width), tok), pl.BlockSpec((rows, width), tok),
                  pl.BlockSpec((rows, width), tok)] + [pl.BlockSpec((1, width), par)] * 5,
        out_specs=pl.BlockSpec((rows, width), tok),
        out_shape=jax.ShapeDtypeStruct((T, D), MXU_DTYPE),
        scratch_shapes=[pltpu.VMEM((pairs, LANES, LANES), F32)],
        compiler_params=_params("parallel", "parallel", "arbitrary"),
        name="wkv7",
    )(rkv, rkv, rkv, ld, a, g, vec(k_k), vec(k_a), vec(r_k), vec(ln_g), vec(ln_b))


def _rwkv_layer(h, S, mix, w_in, w0, w1, w2, a0, a1, a2, g1, g2, k_k, k_a, r_k,
                ln_g, ln_b, w_out, g, b, alpha):
    T, D = h.shape
    rkv = _rwkv_rkv(h, mix[jnp.array([0, 2, 3])].reshape(3, 1, D), w_in.astype(MXU_DTYPE), S)
    ld, a, gate = _rwkv_lora(h, mix[jnp.array([1, 4, 5])], w0, a0, w1, w2, a1, a2, g1, g2, S)
    z = _wkv(rkv, ld, a, gate, k_k, k_a, r_k, ln_g, ln_b, T // S, S)
    return _proj_res_ln(z, w_out.astype(MXU_DTYPE), jnp.zeros((D,), F32), h, g, b, alpha)


def _swa_qkv_kernel(x_ref, w_ref, b_ref, q_ref, k_ref, v_ref, *, q_width, kv_width, scale):
    y = jnp.dot(x_ref[...].astype(MXU_DTYPE), w_ref[...], preferred_element_type=F32) + b_ref[...]
    q_ref[...] = (y[:, :q_width] * scale).astype(q_ref.dtype)
    k_ref[...] = y[:, q_width:q_width + kv_width].astype(k_ref.dtype)
    v_ref[...] = y[:, q_width + kv_width:].astype(v_ref.dtype)


def _swa_qkv(x, w_in, b_in, Hq, Hk):
    T, D = x.shape
    Dh = SWA_HEAD
    qw = Hq * Dh

    def dup(t):
        t3 = t[..., qw:].reshape(t.shape[:-1] + (2, Hk, Dh))
        return jnp.concatenate([t3, t3], -1).reshape(t.shape[:-1] + (2 * Hk * 2 * Dh,))

    w = jnp.concatenate([w_in[:, :qw], dup(w_in)], -1).astype(MXU_DTYPE)
    bias = jnp.concatenate([b_in[:qw], dup(b_in)], -1).reshape(1, -1)
    kvw = Hk * 2 * Dh
    tm = _tile(T, 512)
    row = lambda i: (i, 0)
    return pl.pallas_call(
        functools.partial(_swa_qkv_kernel, q_width=qw, kv_width=kvw, scale=Dh ** -0.5),
        grid=(T // tm,),
        in_specs=[pl.BlockSpec((tm, D), row), _resident(w.shape), _resident(bias.shape)],
        out_specs=[pl.BlockSpec((tm, qw), row), pl.BlockSpec((tm, kvw), row),
                   pl.BlockSpec((tm, kvw), row)],
        out_shape=[jax.ShapeDtypeStruct((T, qw), MXU_DTYPE),
                   jax.ShapeDtypeStruct((T, kvw), MXU_DTYPE),
                   jax.ShapeDtypeStruct((T, kvw), MXU_DTYPE)],
        compiler_params=_params("parallel"),
        name="swa_qkv",
    )(x, w, bias)


def _swa_bias_kernel(bucket_ref, window_ref, rel_ref, sinks_ref, o_ref, sink_rows_ref, *, buckets):
    h = pl.program_id(0)
    bucket = bucket_ref[...]
    acc = jnp.zeros(bucket.shape, F32)
    for n in range(buckets):
        acc = jnp.where(bucket == n, rel_ref[n, h], acc)
    o_ref[0] = jnp.where(window_ref[...] > 0, acc, NEG_INF)
    sink_rows_ref[0] = jnp.full(sink_rows_ref.shape[1:], sinks_ref[h], F32)


def _t5_bucket(dist):
    max_exact = REL_BUCKETS // 2
    n = jnp.maximum(dist, 0)
    nf = jnp.maximum(n, 1).astype(F32)
    large = max_exact + (jnp.log(nf / max_exact) / math.log(SWA_WINDOW / max_exact)
                         * (REL_BUCKETS - max_exact)).astype(jnp.int32)
    large = jnp.minimum(large, REL_BUCKETS - 1)
    return jnp.where(n < max_exact, n, large)


def _swa_bias(rel_bias, sinks):
    W = SWA_WINDOW
    Hq = rel_bias.shape[1]
    dist = jnp.arange(W)[:, None] + W - jnp.arange(2 * W)[None, :]
    bucket = _t5_bucket(dist).astype(jnp.int32)
    in_window = ((dist >= 0) & (dist < W)).astype(jnp.int32)
    return pl.pallas_call(
        functools.partial(_swa_bias_kernel, buckets=REL_BUCKETS),
        grid=(Hq,),
        in_specs=[_resident((W, 2 * W)), _resident((W, 2 * W)),
                  pl.BlockSpec(memory_space=pltpu.SMEM), pl.BlockSpec(memory_space=pltpu.SMEM)],
        out_specs=[pl.BlockSpec((1, W, 2 * W), lambda h: (h, 0, 0)),
                   pl.BlockSpec((1, W, LANES), lambda h: (h, 0, 0))],
        out_shape=[jax.ShapeDtypeStruct((Hq, W, 2 * W), F32),
                   jax.ShapeDtypeStruct((Hq, W, LANES), F32)],
        compiler_params=_params("arbitrary"),
        name="swa_bias",
    )(bucket, in_window, rel_bias, sinks)


def _swa_attn_kernel(q_ref, kc_ref, kp_ref, vc_ref, vp_ref, bias_ref, sink_ref, o_ref,
                     *, kv_heads, group):
    W = SWA_WINDOW
    i = pl.program_id(1)
    lane = lax.broadcasted_iota(jnp.int32, (1, LANES), 1)
    half0 = lane < SWA_HEAD
    rows = group * W
    col = lax.broadcasted_iota(jnp.int32, (rows, 2 * W), 1)
    key_ok = (col >= W) | (i > 0)
    scores = []
    for kv in range(kv_heads):
        ksl = slice(kv * LANES, (kv + 1) * LANES)
        k_band = jnp.concatenate([kp_ref[:, ksl], kc_ref[:, ksl]], 0)
        parts = []
        for pair in range(group // 2):
            blk = kv * (group // 2) + pair
            qp = q_ref[:, blk * LANES:(blk + 1) * LANES]
            zero = jnp.zeros_like(qp)
            parts += [jnp.where(half0, qp, zero), jnp.where(half0, zero, qp)]
        scores.append(_mm_nt(jnp.concatenate(parts, 0), k_band))
    for kv in range(kv_heads):
        ksl = slice(kv * LANES, (kv + 1) * LANES)
        v_band = jnp.concatenate([vp_ref[:, ksl], vc_ref[:, ksl]], 0)
        heads = slice(kv * group, (kv + 1) * group)
        s = scores[kv] + bias_ref[heads].reshape(rows, 2 * W)
        s = jnp.where(key_ok, s, NEG_INF)
        sink = sink_ref[heads].reshape(rows, LANES)
        m = jnp.maximum(jnp.max(s, -1, keepdims=True), sink)
        p = jnp.exp(s - jnp.concatenate([m] * (2 * W // LANES), -1))
        denom = jnp.sum(p, -1, keepdims=True) + jnp.exp(sink - m)
        o = jnp.dot(p.astype(MXU_DTYPE), v_band, preferred_element_type=F32) / denom
        for pair in range(group // 2):
            blk = kv * (group // 2) + pair
            o0 = o[(2 * pair) * W:(2 * pair + 1) * W]
            o1 = o[(2 * pair + 1) * W:(2 * pair + 2) * W]
            o_ref[:, blk * LANES:(blk + 1) * LANES] = jnp.where(half0, o0, o1).astype(o_ref.dtype)


def _swa_attention(q, kd, vd, bias, sink_rows, B, S, Hq, Hk):
    T = B * S
    W = SWA_WINDOW
    nb = S // W
    qw = Hq * SWA_HEAD
    kvw = Hk * LANES
    cur = lambda b, i: (b * nb + i, 0)
    prev = lambda b, i: (b * nb + jnp.maximum(i - 1, 0), 0)
    return pl.pallas_call(
        functools.partial(_swa_attn_kernel, kv_heads=Hk, group=Hq // Hk),
        grid=(B, nb),
        in_specs=[pl.BlockSpec((W, qw), cur),
                  pl.BlockSpec((W, kvw), cur), pl.BlockSpec((W, kvw), prev),
                  pl.BlockSpec((W, kvw), cur), pl.BlockSpec((W, kvw), prev),
                  _resident(bias.shape), _resident(sink_rows.shape)],
        out_specs=pl.BlockSpec((W, qw), cur),
        out_shape=jax.ShapeDtypeStruct((T, qw), MXU_DTYPE),
        compiler_params=_params("parallel", "arbitrary"),
        name="swa_attn",
    )(q, kd, kd, vd, vd, bias, sink_rows)


def _swa_layer(h, S, w_in, b_in, sinks, w_out, b_out, rel_bias, g, b, alpha):
    T, D = h.shape
    Hq = D // SWA_HEAD
    Hk = (w_in.shape[1] - D) // (2 * SWA_HEAD)
    assert (Hq // Hk) % 2 == 0
    q, kd, vd = _swa_qkv(h, w_in, b_in, Hq, Hk)
    bias, sink_rows = _swa_bias(rel_bias, sinks)
    o = _swa_attention(q, kd, vd, bias, sink_rows, T // S, S, Hq, Hk)
    return _proj_res_ln(o, w_out.astype(MXU_DTYPE), b_out, h, g, b, alpha)


def kernel(x, mla_w_in, mla_q_norm, mla_kv_norm, mla_w_q_b, mla_w_kv_b, mla_w_out, rwkv_mix, rwkv_w_in, rwkv_w0, rwkv_w1, rwkv_w2, rwkv_a0, rwkv_a1, rwkv_a2, rwkv_g1, rwkv_g2, rwkv_k_k, rwkv_k_a, rwkv_r_k, rwkv_ln_g, rwkv_ln_b, rwkv_w_out, swa_w_in, swa_b_in, swa_sinks, swa_w_out, swa_b_out, rel_bias, ln_g, ln_b, mlp_up, mlp_down):
    B, S, D = x.shape
    depth = ln_g.shape[0]
    alpha = (2.0 * depth) ** 0.25
    h = x.reshape(B * S, D)
    mlp_up_b = mlp_up.astype(MXU_DTYPE)
    mlp_down_b = mlp_down.astype(MXU_DTYPE)
    for i in range(depth):
        j = i // 3
        kind = i % 3
        g0, b0 = ln_g[i, 0], ln_b[i, 0]
        if kind == 0:
            h = _mla_layer(h, S, mla_w_in[j], mla_q_norm[j], mla_kv_norm[j], mla_w_q_b[j],
                           mla_w_kv_b[j], mla_w_out[j], g0, b0, alpha)
        elif kind == 1:
            h = _rwkv_layer(h, S, rwkv_mix[j], rwkv_w_in[j], rwkv_w0[j], rwkv_w1[j], rwkv_w2[j],
                            rwkv_a0[j], rwkv_a1[j], rwkv_a2[j], rwkv_g1[j], rwkv_g2[j],
                            rwkv_k_k[j], rwkv_k_a[j], rwkv_r_k[j].reshape(-1), rwkv_ln_g[j],
                            rwkv_ln_b[j], rwkv_w_out[j], g0, b0, alpha)
        else:
            h = _swa_layer(h, S, swa_w_in[j], swa_b_in[j], swa_sinks[j], swa_w_out[j],
                           swa_b_out[j], rel_bias, g0, b0, alpha)
        h = _mlp_res_ln(h, mlp_up_b, mlp_down_b, i, ln_g[i, 1], ln_b[i, 1], alpha)
    return h.reshape(B, S, D)
```

```python
import functools
import math

import jax
import jax.numpy as jnp
from jax import lax
from jax.experimental import pallas as pl
from jax.experimental.pallas import tpu as pltpu

F32 = jnp.float32
MXU_DTYPE = jnp.bfloat16

LN_EPS = 1e-5
RMS_EPS = 1e-6
NEG_INF = -1e30
ROPE_THETA = 10000.0

LANES = 128
SUBLANES = 8
VMEM_LIMIT = 56 * 1024 * 1024

MLA_NOPE = 128
MLA_ROPE = 64
MLA_V = 128
RWKV_HEAD = 64
RWKV_GN_EPS = 64e-5
RWKV_CHUNK = 64
SWA_HEAD = 64
SWA_WINDOW = 128
REL_BUCKETS = 32

ROW_TILE = 512
MLP_BLOCK = 2048
MLP_CHUNK = 1024
FLASH_SUB = 512
FLASH_STREAMS = 8
WKV_BLOCKS = 16
WKV_CHUNKS = 2


def _params(*sem):
    return pltpu.CompilerParams(dimension_semantics=sem, vmem_limit_bytes=VMEM_LIMIT)


def _tile(n, want):
    t = min(n, want)
    assert n % t == 0, (n, t)
    return t


def _resident(shape):
    nd = len(shape)
    return pl.BlockSpec(shape, lambda *_: (0,) * nd, pipeline_mode=pl.Buffered(1))


def _mm(a, b):
    return jnp.dot(a.astype(MXU_DTYPE), b.astype(MXU_DTYPE), preferred_element_type=F32)


def _mm_nt(a, b):
    return lax.dot_general(a.astype(MXU_DTYPE), b.astype(MXU_DTYPE),
                           (((1,), (1,)), ((), ())), preferred_element_type=F32)


def _mm_tn(a, b):
    return lax.dot_general(a.astype(MXU_DTYPE), b.astype(MXU_DTYPE),
                           (((0,), (0,)), ((), ())), preferred_element_type=F32)


def _layer_norm(y, g, b):
    mu = jnp.mean(y, -1, keepdims=True)
    d = y - mu
    var = jnp.mean(d * d, -1, keepdims=True)
    return d * lax.rsqrt(var + LN_EPS) * g + b


def _rms_norm(y, g):
    return y * lax.rsqrt(jnp.mean(y * y, -1, keepdims=True) + RMS_EPS) * g


def _sigmoid(z):
    return 1.0 / (1.0 + jnp.exp(-z))


def _proj_res_ln_kernel(a_ref, w_ref, bias_ref, x_ref, g_ref, b_ref, o_ref, *, alpha):
    y = jnp.dot(a_ref[...], w_ref[...], preferred_element_type=F32)
    y = y + bias_ref[...] + alpha * x_ref[...]
    o_ref[...] = _layer_norm(y, g_ref[...], b_ref[...])


def _proj_res_ln(a, w, bias, x, g, b, alpha):
    T, K = a.shape
    D = w.shape[1]
    tm = _tile(T, ROW_TILE)
    return pl.pallas_call(
        functools.partial(_proj_res_ln_kernel, alpha=alpha),
        grid=(T // tm,),
        in_specs=[pl.BlockSpec((tm, K), lambda i: (i, 0)),
                  _resident((K, D)), _resident((1, D)),
                  pl.BlockSpec((tm, D), lambda i: (i, 0)),
                  _resident((1, D)), _resident((1, D))],
        out_specs=pl.BlockSpec((tm, D), lambda i: (i, 0)),
        out_shape=jax.ShapeDtypeStruct((T, D), F32),
        compiler_params=_params("parallel"),
        name="proj_res_ln",
    )(a, w, bias.reshape(1, D), x, g.reshape(1, D), b.reshape(1, D))


def _mlp_kernel(x_ref, wu_ref, wd_ref, g_ref, b_ref, o_ref, xb_ref, *, alpha, inner):
    j = pl.program_id(1)
    width = wu_ref.shape[1] // inner

    def chunk(xb, n):
        cols = slice(n * width, (n + 1) * width)
        u = jnp.maximum(jnp.dot(xb, wu_ref[:, cols], preferred_element_type=F32), 0.0)
        u = (u * u).astype(MXU_DTYPE)
        return jnp.dot(u, wd_ref[cols, :], preferred_element_type=F32)

    @pl.when(j == 0)
    def _():
        x = x_ref[...]
        xb = x.astype(MXU_DTYPE)
        xb_ref[...] = xb
        o_ref[...] = alpha * x + chunk(xb, 0)
        for n in range(1, inner):
            o_ref[...] += chunk(xb_ref[...], n)

    @pl.when(j > 0)
    def _():
        for n in range(inner):
            o_ref[...] += chunk(xb_ref[...], n)

    @pl.when(j == pl.num_programs(1) - 1)
    def _():
        o_ref[...] = _layer_norm(o_ref[...], g_ref[...], b_ref[...])


def _mlp_res_ln(x, wu_all, wd_all, layer, g, b, alpha):
    T, D = x.shape
    Hd = wu_all.shape[2]
    tm = _tile(T, ROW_TILE)
    th = _tile(Hd, MLP_BLOCK)
    inner = th // _tile(th, MLP_CHUNK)
    return pl.pallas_call(
        functools.partial(_mlp_kernel, alpha=alpha, inner=inner),
        grid=(T // tm, Hd // th),
        in_specs=[pl.BlockSpec((tm, D), lambda i, j: (i, 0)),
                  pl.BlockSpec((None, D, th), lambda i, j: (layer, 0, j)),
                  pl.BlockSpec((None, th, D), lambda i, j: (layer, j, 0)),
                  _resident((1, D)), _resident((1, D))],
        out_specs=pl.BlockSpec((tm, D), lambda i, j: (i, 0)),
        out_shape=jax.ShapeDtypeStruct((T, D), F32),
        scratch_shapes=[pltpu.VMEM((tm, D), MXU_DTYPE)],
        compiler_params=_params("parallel", "arbitrary"),
        name="mlp_res_ln",
    )(x, wu_all, wd_all, g.reshape(1, D), b.reshape(1, D))


def _mla_proj_kernel(x_ref, wqa_ref, wkva_ref, wkr_ref, wkrr_ref, qg_ref, kvg_ref,
                     wqn_ref, wqr_ref, wqrr_ref, wkn_ref, wv_ref, cos_ref, sin_ref,
                     qn_ref, qr_ref, kn_ref, v_ref, kr_ref, *, heads, scale):
    xb = x_ref[...].astype(MXU_DTYPE)
    cos = cos_ref[...]
    sin = sin_ref[...]
    q_lat = jnp.dot(xb, wqa_ref[...], preferred_element_type=F32)
    kv_lat = jnp.dot(xb, wkva_ref[...], preferred_element_type=F32)
    kr = jnp.dot(xb, wkr_ref[...], preferred_element_type=F32)
    krr = jnp.dot(xb, wkrr_ref[...], preferred_element_type=F32)
    kr_ref[...] = (kr * cos + krr * sin).astype(kr_ref.dtype)

    qn = _rms_norm(q_lat, qg_ref[...]).astype(MXU_DTYPE)
    qn_ref[...] = (jnp.dot(qn, wqn_ref[...], preferred_element_type=F32) * scale).astype(qn_ref.dtype)
    qr = jnp.dot(qn, wqr_ref[...], preferred_element_type=F32)
    qrr = jnp.dot(qn, wqrr_ref[...], preferred_element_type=F32)
    for h in range(heads):
        sl = slice(h * LANES, (h + 1) * LANES)
        qr_ref[:, sl] = ((qr[:, sl] * cos + qrr[:, sl] * sin) * scale).astype(qr_ref.dtype)

    kvn = _rms_norm(kv_lat, kvg_ref[...]).astype(MXU_DTYPE)
    kn_ref[...] = jnp.dot(kvn, wkn_ref[...], preferred_element_type=F32).astype(kn_ref.dtype)
    v_ref[...] = jnp.dot(kvn, wv_ref[...], preferred_element_type=F32).astype(v_ref.dtype)


def _rot_half_cols(w):
    half = w.shape[-1] // 2
    return jnp.concatenate([-w[..., half:], w[..., :half]], -1)


def _pad_lanes(w, width):
    pad = [(0, 0)] * (w.ndim - 1) + [(0, width - w.shape[-1])]
    return jnp.pad(w, pad)


def _mla_project(x, w_in, q_norm, kv_norm, w_q_b, w_kv_b, S):
    T, D = x.shape
    q_rank = q_norm.shape[-1]
    kv_rank = kv_norm.shape[-1]
    H = w_q_b.shape[1] // (MLA_NOPE + MLA_ROPE)
    scale = (MLA_NOPE + MLA_ROPE) ** -0.5 * math.log2(math.e)

    wqa = w_in[:, :q_rank].astype(MXU_DTYPE)
    wkva = w_in[:, q_rank:q_rank + kv_rank].astype(MXU_DTYPE)
    wkr_raw = w_in[:, q_rank + kv_rank:]
    wkr = _pad_lanes(wkr_raw, LANES).astype(MXU_DTYPE)
    wkrr = _pad_lanes(_rot_half_cols(wkr_raw), LANES).astype(MXU_DTYPE)
    wq3 = w_q_b.reshape(q_rank, H, MLA_NOPE + MLA_ROPE)
    wqn = wq3[:, :, :MLA_NOPE].reshape(q_rank, H * MLA_NOPE).astype(MXU_DTYPE)
    wq_rope = wq3[:, :, MLA_NOPE:]
    wqr = _pad_lanes(wq_rope, LANES).reshape(q_rank, H * LANES).astype(MXU_DTYPE)
    wqrr = _pad_lanes(_rot_half_cols(wq_rope), LANES).reshape(q_rank, H * LANES).astype(MXU_DTYPE)
    wkv3 = w_kv_b.reshape(kv_rank, H, MLA_NOPE + MLA_V)
    wkn = wkv3[:, :, :MLA_NOPE].reshape(kv_rank, H * MLA_NOPE).astype(MXU_DTYPE)
    wv = wkv3[:, :, MLA_NOPE:].reshape(kv_rank, H * MLA_V).astype(MXU_DTYPE)

    half = MLA_ROPE // 2
    inv = ROPE_THETA ** (-jnp.arange(half, dtype=F32) / half)
    ang = jnp.arange(S, dtype=F32)[:, None] * inv[None, :]
    cos = _pad_lanes(jnp.concatenate([jnp.cos(ang), jnp.cos(ang)], -1), LANES)
    sin = _pad_lanes(jnp.concatenate([jnp.sin(ang), jnp.sin(ang)], -1), LANES)

    tm = _tile(S, ROW_TILE)
    ns = S // tm
    row = lambda i: (i, 0)
    pos = lambda i: (i % ns, 0)
    wide = jax.ShapeDtypeStruct((T, H * LANES), MXU_DTYPE)
    outs = pl.pallas_call(
        functools.partial(_mla_proj_kernel, heads=H, scale=scale),
        grid=(T // tm,),
        in_specs=[pl.BlockSpec((tm, D), row),
                  _resident(wqa.shape), _resident(wkva.shape), _resident(wkr.shape),
                  _resident(wkrr.shape), _resident((1, q_rank)), _resident((1, kv_rank)),
                  _resident(wqn.shape), _resident(wqr.shape), _resident(wqrr.shape),
                  _resident(wkn.shape), _resident(wv.shape),
                  pl.BlockSpec((tm, LANES), pos), pl.BlockSpec((tm, LANES), pos)],
        out_specs=[pl.BlockSpec((tm, H * LANES), row)] * 4 + [pl.BlockSpec((tm, LANES), row)],
        out_shape=[wide, wide, wide, wide, jax.ShapeDtypeStruct((T, LANES), MXU_DTYPE)],
        compiler_params=_params("parallel"),
        name="mla_proj",
    )(x, wqa, wkva, wkr, wkrr, q_norm.reshape(1, -1), kv_norm.reshape(1, -1),
      wqn, wqr, wqrr, wkn, wv, cos, sin)
    return outs, H


def _flash_kernel(qn_ref, qr_ref, kn_ref, kr_ref, v_ref, o_ref, m_scr, l_scr, acc_scr,
                  *, sub, streams):
    blk = pl.program_id(2)
    q = jnp.concatenate([qn_ref[...], qr_ref[...]], -1)
    m_scr[...] = jnp.full(m_scr.shape, NEG_INF, F32)
    l_scr[...] = jnp.zeros(l_scr.shape, F32)
    acc_scr[...] = jnp.zeros(acc_scr.shape, F32)

    def kv_rows(j):
        return pl.ds(pl.multiple_of(j * sub, sub), sub)

    def keys(j):
        return jnp.concatenate([kn_ref[kv_rows(j), :], kr_ref[kv_rows(j), :]], -1)

    def softmax_pv(s, a, j, diagonal):
        rows = slice(a * sub, (a + 1) * sub)
        if diagonal:
            r = lax.broadcasted_iota(jnp.int32, s.shape, 0)
            c = lax.broadcasted_iota(jnp.int32, s.shape, 1)
            s = jnp.where(c <= r, s, NEG_INF)
        m_prev = m_scr[rows, :]
        m_new = jnp.maximum(m_prev, jnp.max(s, -1, keepdims=True))
        p = jnp.exp2(s - jnp.concatenate([m_new] * (sub // LANES), -1))
        corr = jnp.exp2(m_prev - m_new)
        m_scr[rows, :] = m_new
        l_scr[rows, :] = corr * l_scr[rows, :] + jnp.sum(p, -1, keepdims=True)
        acc_scr[rows, :] = corr * acc_scr[rows, :] + jnp.dot(
            p.astype(MXU_DTYPE), v_ref[kv_rows(j), :], preferred_element_type=F32)

    def body(j, carry):
        k = keys(j)
        s_next = _mm_nt(q[:sub], k)
        for a in range(streams):
            s = s_next
            if a + 1 < streams:
                s_next = _mm_nt(q[(a + 1) * sub:(a + 2) * sub], k)
            softmax_pv(s, a, j, False)
        return carry

    lax.fori_loop(0, streams * blk, body, 0)
    for b in range(streams):
        j = streams * blk + b
        s_part = _mm_nt(q[b * sub:], keys(j))
        for a in range(b, streams):
            softmax_pv(s_part[(a - b) * sub:(a - b + 1) * sub], a, j, a == b)
    o_ref[...] = (acc_scr[...] / l_scr[...]).astype(o_ref.dtype)


def _flash_attention(qn, qr, kn, kr, v, B, S, H):
    T = B * S
    sub = _tile(S, FLASH_SUB)
    streams = _tile(S // sub, FLASH_STREAMS)
    tq = sub * streams
    nq = S // tq
    qmap = lambda b, h, i: (b * nq + i, h)
    kmap = lambda b, h, i: (b, h)
    stat = pltpu.VMEM((tq, LANES), F32)
    return pl.pallas_call(
        functools.partial(_flash_kernel, sub=sub, streams=streams),
        grid=(B, H, nq),
        in_specs=[pl.BlockSpec((tq, LANES), qmap), pl.BlockSpec((tq, LANES), qmap),
                  pl.BlockSpec((S, LANES), kmap),
                  pl.BlockSpec((S, LANES), lambda b, h, i: (b, 0)),
                  pl.BlockSpec((S, LANES), kmap)],
        out_specs=pl.BlockSpec((tq, LANES), qmap),
        out_shape=jax.ShapeDtypeStruct((T, H * MLA_V), MXU_DTYPE),
        scratch_shapes=[stat, stat, stat],
        compiler_params=_params("parallel", "parallel", "arbitrary"),
        name="mla_flash",
    )(qn, qr, kn, kr, v)


def _mla_layer(h, S, w_in, q_norm, kv_norm, w_q_b, w_kv_b, w_out, g, b, alpha):
    T, D = h.shape
    (qn, qr, kn, v, kr), H = _mla_project(h, w_in, q_norm, kv_norm, w_q_b, w_kv_b, S)
    o = _flash_attention(qn, qr, kn, kr, v, T // S, S, H)
    return _proj_res_ln(o, w_out.astype(MXU_DTYPE), jnp.zeros((D,), F32), h, g, b, alpha)


def _shift_delta(x, prev_rows, is_seq_start):
    prev = jnp.where(is_seq_start, 0.0, prev_rows[SUBLANES - 1:SUBLANES, :])
    shifted = pltpu.roll(x, 1, 0)
    first = lax.broadcasted_iota(jnp.int32, x.shape, 0) == 0
    return jnp.where(first, prev, shifted) - x


def _rwkv_rkv_kernel(x_ref, prev_ref, mix_ref, w_ref, o_ref, *, tiles_per_seq):
    i = pl.program_id(1)
    x = x_ref[...]
    xx = _shift_delta(x, prev_ref[...], i % tiles_per_seq == 0)
    xm = (x + xx * mix_ref[0]).astype(MXU_DTYPE)
    o_ref[0] = jnp.dot(xm, w_ref[0], preferred_element_type=F32)


def _rwkv_rkv(x, mix3, w3, S):
    T, D = x.shape
    tm = _tile(S, ROW_TILE)
    per8 = tm // SUBLANES
    return pl.pallas_call(
        functools.partial(_rwkv_rkv_kernel, tiles_per_seq=S // tm),
        grid=(3, T // tm),
        in_specs=[pl.BlockSpec((tm, D), lambda n, i: (i, 0)),
                  pl.BlockSpec((SUBLANES, D), lambda n, i: (jnp.maximum(i * per8 - 1, 0), 0)),
                  pl.BlockSpec((1, 1, D), lambda n, i: (n, 0, 0)),
                  pl.BlockSpec((1, D, D), lambda n, i: (n, 0, 0))],
        out_specs=pl.BlockSpec((1, tm, D), lambda n, i: (n, i, 0)),
        out_shape=jax.ShapeDtypeStruct((3, T, D), F32),
        compiler_params=_params("parallel", "parallel"),
        name="rwkv_rkv",
    )(x, x, mix3, w3)


def _rwkv_lora_kernel(x_ref, prev_ref, mix_ref, w0_ref, a0_ref, w1_ref, w2_ref, a1_ref,
                      a2_ref, g1_ref, g2_ref, ld_ref, a_ref, g_ref, *, tiles_per_seq):
    i = pl.program_id(0)
    x = x_ref[...]
    xx = _shift_delta(x, prev_ref[...], i % tiles_per_seq == 0)
    xw = (x + xx * mix_ref[0:1, :]).astype(MXU_DTYPE)
    xa = (x + xx * mix_ref[1:2, :]).astype(MXU_DTYPE)
    xg = (x + xx * mix_ref[2:3, :]).astype(MXU_DTYPE)
    hw = jnp.tanh(jnp.dot(xw, w1_ref[...], preferred_element_type=F32)).astype(MXU_DTYPE)
    u = w0_ref[...] + jnp.dot(hw, w2_ref[...], preferred_element_type=F32)
    ld_ref[...] = -math.exp(-0.5) * _sigmoid(u)
    ha = jnp.dot(xa, a1_ref[...], preferred_element_type=F32).astype(MXU_DTYPE)
    a_ref[...] = _sigmoid(a0_ref[...] + jnp.dot(ha, a2_ref[...], preferred_element_type=F32))
    hg = _sigmoid(jnp.dot(xg, g1_ref[...], preferred_element_type=F32)).astype(MXU_DTYPE)
    g_ref[...] = jnp.dot(hg, g2_ref[...], preferred_element_type=F32)


def _pad_rank(w_down, w_up):
    r = w_down.shape[1]
    rp = -(-r // LANES) * LANES
    return (jnp.pad(w_down, ((0, 0), (0, rp - r))).astype(MXU_DTYPE),
            jnp.pad(w_up, ((0, rp - r), (0, 0))).astype(MXU_DTYPE))


def _rwkv_lora(x, mix3, w0, a0, w1, w2, a1, a2, g1, g2, S):
    T, D = x.shape
    tm = _tile(S, ROW_TILE)
    per8 = tm // SUBLANES
    row = lambda i: (i, 0)
    out = jax.ShapeDtypeStruct((T, D), F32)
    w1p, w2p = _pad_rank(w1, w2)
    a1p, a2p = _pad_rank(a1, a2)
    g1p, g2p = _pad_rank(g1, g2)
    return pl.pallas_call(
        functools.partial(_rwkv_lora_kernel, tiles_per_seq=S // tm),
        grid=(T // tm,),
        in_specs=[pl.BlockSpec((tm, D), row),
                  pl.BlockSpec((SUBLANES, D), lambda i: (jnp.maximum(i * per8 - 1, 0), 0)),
                  _resident((3, D)), _resident((1, D)), _resident((1, D)),
                  _resident(w1p.shape), _resident(w2p.shape), _resident(a1p.shape),
                  _resident(a2p.shape), _resident(g1p.shape), _resident(g2p.shape)],
        out_specs=[pl.BlockSpec((tm, D), row)] * 3,
        out_shape=[out, out, out],
        compiler_params=_params("parallel"),
        name="rwkv_lora",
    )(x, x, mix3, w0.reshape(1, D), a0.reshape(1, D), w1p, w2p, a1p, a2p, g1p, g2p)


def _wkv_kernel(r_ref, k_ref, v_ref, ld_ref, a_ref, g_ref, kk_ref, ka_ref, rk_ref,
                lng_ref, lnb_ref, z_ref, state_ref, *, pairs, chunk, chunks):
    L = chunk

    @pl.when(pl.program_id(2) == 0)
    def _():
        state_ref[...] = jnp.zeros_like(state_ref)

    lane = lax.broadcasted_iota(jnp.int32, (1, LANES), 1)
    head0 = lane < RWKV_HEAD

    def head_sum(t):
        s0 = jnp.sum(jnp.where(head0, t, 0.0), -1, keepdims=True)
        s1 = jnp.sum(jnp.where(head0, 0.0, t), -1, keepdims=True)
        return jnp.where(head0, s0, s1)

    def stack(t):
        return jnp.concatenate([jnp.where(head0, t, 0.0), jnp.where(head0, 0.0, t)], 0)

    ld_all = ld_ref[...]
    rows_t = lax.broadcasted_iota(jnp.int32, (chunks * L, chunks * L), 0)
    cols_t = lax.broadcasted_iota(jnp.int32, (chunks * L, chunks * L), 1)
    tri = ((rows_t >= cols_t) & (rows_t // L == cols_t // L)).astype(MXU_DTYPE)
    ld_hi = ld_all.astype(MXU_DTYPE)
    rem = ld_all - ld_hi.astype(F32)
    ld_mid = rem.astype(MXU_DTYPE)
    ld_lo = (rem - ld_mid.astype(F32)).astype(MXU_DTYPE)
    cum_all = (jnp.dot(tri, ld_hi, preferred_element_type=F32)
               + jnp.dot(tri, ld_mid, preferred_element_type=F32)
               + jnp.dot(tri, ld_lo, preferred_element_type=F32))

    row_p = lax.broadcasted_iota(jnp.int32, (L, LANES), 0)
    col_p = lax.broadcasted_iota(jnp.int32, (L, LANES), 1) % RWKV_HEAD
    strict = row_p > col_p
    incl = row_p >= col_p
    eye_p = (row_p == col_p).astype(F32)
    same_head = ((lax.broadcasted_iota(jnp.int32, (LANES, LANES), 0) // RWKV_HEAD)
                 == (lax.broadcasted_iota(jnp.int32, (LANES, LANES), 1) // RWKV_HEAD))

    units = [(slice(c * L, (c + 1) * L), slice(p * LANES, (p + 1) * LANES))
             for c in range(chunks) for p in range(pairs)]
    n_units = len(units)
    lhs, rhs, v_b, v2, tails, w_chunk = [], [], [], [], [], []
    for rows, sl in units:
        r = r_ref[rows, sl]
        k = k_ref[rows, sl]
        a = a_ref[rows, sl]
        ld = ld_all[rows, sl]
        cum = cum_all[rows, sl]
        kk = k * kk_ref[:, sl]
        kk = kk / jnp.maximum(jnp.sqrt(head_sum(kk * kk)), 1e-12)
        k_mod = k * (1.0 + (a - 1.0) * ka_ref[:, sl])
        bb = kk * a
        cum_last = cum[L - 1:L, :]
        e_neg = jnp.exp(-cum)
        e_tail = jnp.exp(cum_last - cum)
        lhs.append(jnp.concatenate([-kk * jnp.exp(cum - ld), r * jnp.exp(cum)],
                                   0).astype(MXU_DTYPE))
        rhs.append(jnp.concatenate([stack(bb * e_neg), stack(k_mod * e_neg)], 0).astype(MXU_DTYPE))
        tails.append(jnp.concatenate([bb * e_tail, k_mod * e_tail], 0).astype(MXU_DTYPE))
        v = v_ref[rows, sl].astype(MXU_DTYPE)
        v_b.append(v)
        v2.append(stack(v))
        w_chunk.append(jnp.exp(cum_last))

    gram = [_mm_nt(lhs[u], rhs[u]) for u in range(n_units)]
    n_ab = [jnp.where(strict, g_[:L, :2 * L], 0.0) for g_ in gram]
    m_ak = [jnp.where(strict, g_[:L, 2 * L:], 0.0).astype(MXU_DTYPE) for g_ in gram]
    n_r = [jnp.concatenate([jnp.where(incl, g_[L:, :2 * L], 0.0),
                            jnp.where(incl, g_[L:, 2 * L:], 0.0)], 1).astype(MXU_DTYPE)
           for g_ in gram]

    levels = int(math.log2(L))
    t_inv = [eye_p + n for n in n_ab]
    n_pow = [_mm(n, stack(n)) for n in n_ab]
    for _ in range(levels - 2):
        both = [_mm(jnp.concatenate([n_pow[u], t_inv[u]], 0), stack(n_pow[u]))
                for u in range(n_units)]
        n_pow = [b_[:L] for b_ in both]
        t_inv = [t_inv[u] + both[u][L:] for u in range(n_units)]
    t_inv = [t_inv[u] + _mm(t_inv[u], stack(n_pow[u])) for u in range(n_units)]
    m_ak_v = [_mm(m_ak[u], v2[u]) for u in range(n_units)]

    state = [state_ref[p] for p in range(pairs)]
    y_all = []
    for c in range(chunks):
        us = [c * pairs + p for p in range(pairs)]
        from_state = [_mm_nt(lhs[u], state[p]) for p, u in enumerate(us)]
        x = [from_state[p][:L] + m_ak_v[u] for p, u in enumerate(us)]
        sa = [_mm(t_inv[u], stack(x[p])).astype(MXU_DTYPE) for p, u in enumerate(us)]
        y_all += [from_state[p][L:] + _mm(n_r[u], jnp.concatenate([stack(sa[p]), v2[u]], 0))
                  for p, u in enumerate(us)]
        update = [_mm_tn(jnp.concatenate([sa[p], v_b[u]], 0), tails[u])
                  for p, u in enumerate(us)]
        state = [state[p] * w_chunk[u] + jnp.where(same_head, update[p], 0.0)
                 for p, u in enumerate(us)]
    for p in range(pairs):
        state_ref[p] = state[p]

    for u, (rows, sl) in enumerate(units):
        y = y_all[u]
        r = r_ref[rows, sl]
        k_mod = k_ref[rows, sl] * (1.0 + (a_ref[rows, sl] - 1.0) * ka_ref[:, sl])
        mu = head_sum(y) * (1.0 / RWKV_HEAD)
        d = y - mu
        var = head_sum(d * d) * (1.0 / RWKV_HEAD)
        yn = d * lax.rsqrt(var + RWKV_GN_EPS) * lng_ref[:, sl] + lnb_ref[:, sl]
        bonus = head_sum(r * k_mod * rk_ref[:, sl]) * v_ref[rows, sl]
        z_ref[rows, sl] = ((yn + bonus) * g_ref[rows, sl]).astype(z_ref.dtype)


def _wkv(rkv, ld, a, g, k_k, k_a, r_k, ln_g, ln_b, B, S):
    _, T, D = rkv.shape
    L = RWKV_CHUNK
    n_pairs = D // LANES
    pairs = _tile(n_pairs, WKV_BLOCKS)
    chunks = _tile(S // L, WKV_CHUNKS)
    width = pairs * LANES
    rows = chunks * L
    nc = S // rows
    tok = lambda b, q, c: (b * nc + c, q)
    par = lambda b, q, c: (0, q)

    def rkv_spec(n):
        return pl.BlockSpec((None, rows, width), lambda b, q, c: (n, b * nc + c, q))

    vec = lambda t: t.reshape(1, D)
    return pl.pallas_call(
        functools.partial(_wkv_kernel, pairs=pairs, chunk=L, chunks=chunks),
        grid=(B, n_pairs // pairs, nc),
        in_specs=[rkv_spec(0), rkv_spec(1), rkv_spec(2),
                  pl.BlockSpec((rows, width), tok), pl.BlockSpec((rows, width), tok),
                  pl.BlockSpec((rows, width), tok)] + [pl.BlockSpec((1, width), par)] * 5,
        out_specs=pl.BlockSpec((rows, width), tok),
        out_shape=jax.ShapeDtypeStruct((T, D), MXU_DTYPE),
        scratch_shapes=[pltpu.VMEM((pairs, LANES, LANES), F32)],
        compiler_params=_params("parallel", "parallel", "arbitrary"),
        name="wkv7",
    )(rkv, rkv, rkv, ld, a, g, vec(k_k), vec(k_a), vec(r_k), vec(ln_g), vec(ln_b))


def _rwkv_layer(h, S, mix, w_in, w0, w1, w2, a0, a1, a2, g1, g2, k_k, k_a, r_k,
                ln_g, ln_b, w_out, g, b, alpha):
    T, D = h.shape
    rkv = _rwkv_rkv(h, mix[jnp.array([0, 2, 3])].reshape(3, 1, D), w_in.astype(MXU_DTYPE), S)
    ld, a, gate = _rwkv_lora(h, mix[jnp.array([1, 4, 5])], w0, a0, w1, w2, a1, a2, g1, g2, S)
    z = _wkv(rkv, ld, a, gate, k_k, k_a, r_k, ln_g, ln_b, T // S, S)
    return _proj_res_ln(z, w_out.astype(MXU_DTYPE), jnp.zeros((D,), F32), h, g, b, alpha)


def _swa_qkv_kernel(x_ref, w_ref, b_ref, q_ref, k_ref, v_ref, *, q_width, kv_width, scale):
    y = jnp.dot(x_ref[...].astype(MXU_DTYPE), w_ref[...], preferred_element_type=F32) + b_ref[...]
    q_ref[...] = (y[:, :q_width] * scale).astype(q_ref.dtype)
    k_ref[...] = y[:, q_width:q_width + kv_width].astype(k_ref.dtype)
    v_ref[...] = y[:, q_width + kv_width:].astype(v_ref.dtype)


def _swa_qkv(x, w_in, b_in, Hq, Hk):
    T, D = x.shape
    Dh = SWA_HEAD
    qw = Hq * Dh

    def dup(t):
        t3 = t[..., qw:].reshape(t.shape[:-1] + (2, Hk, Dh))
        return jnp.concatenate([t3, t3], -1).reshape(t.shape[:-1] + (2 * Hk * 2 * Dh,))

    w = jnp.concatenate([w_in[:, :qw], dup(w_in)], -1).astype(MXU_DTYPE)
    bias = jnp.concatenate([b_in[:qw], dup(b_in)], -1).reshape(1, -1)
    kvw = Hk * 2 * Dh
    tm = _tile(T, ROW_TILE)
    row = lambda i: (i, 0)
    return pl.pallas_call(
        functools.partial(_swa_qkv_kernel, q_width=qw, kv_width=kvw, scale=Dh ** -0.5),
        grid=(T // tm,),
        in_specs=[pl.BlockSpec((tm, D), row), _resident(w.shape), _resident(bias.shape)],
        out_specs=[pl.BlockSpec((tm, qw), row), pl.BlockSpec((tm, kvw), row),
                   pl.BlockSpec((tm, kvw), row)],
        out_shape=[jax.ShapeDtypeStruct((T, qw), MXU_DTYPE),
                   jax.ShapeDtypeStruct((T, kvw), MXU_DTYPE),
                   jax.ShapeDtypeStruct((T, kvw), MXU_DTYPE)],
        compiler_params=_params("parallel"),
        name="swa_qkv",
    )(x, w, bias)


def _swa_bias_kernel(bucket_ref, window_ref, rel_ref, sinks_ref, o_ref, sink_rows_ref, *, buckets):
    h = pl.program_id(0)
    bucket = bucket_ref[...]
    acc = jnp.zeros(bucket.shape, F32)
    for n in range(buckets):
        acc = jnp.where(bucket == n, rel_ref[n, h], acc)
    o_ref[0] = jnp.where(window_ref[...] > 0, acc, NEG_INF)
    sink_rows_ref[0] = jnp.full(sink_rows_ref.shape[1:], sinks_ref[h], F32)


def _t5_bucket(dist):
    max_exact = REL_BUCKETS // 2
    n = jnp.maximum(dist, 0)
    nf = jnp.maximum(n, 1).astype(F32)
    large = max_exact + (jnp.log(nf / max_exact) / math.log(SWA_WINDOW / max_exact)
                         * (REL_BUCKETS - max_exact)).astype(jnp.int32)
    large = jnp.minimum(large, REL_BUCKETS - 1)
    return jnp.where(n < max_exact, n, large)


def _swa_bias(rel_bias, sinks):
    W = SWA_WINDOW
    Hq = rel_bias.shape[1]
    dist = jnp.arange(W)[:, None] + W - jnp.arange(2 * W)[None, :]
    bucket = _t5_bucket(dist).astype(jnp.int32)
    in_window = ((dist >= 0) & (dist < W)).astype(jnp.int32)
    return pl.pallas_call(
        functools.partial(_swa_bias_kernel, buckets=REL_BUCKETS),
        grid=(Hq,),
        in_specs=[_resident((W, 2 * W)), _resident((W, 2 * W)),
                  pl.BlockSpec(memory_space=pltpu.SMEM), pl.BlockSpec(memory_space=pltpu.SMEM)],
        out_specs=[pl.BlockSpec((1, W, 2 * W), lambda h: (h, 0, 0)),
                   pl.BlockSpec((1, W, LANES), lambda h: (h, 0, 0))],
        out_shape=[jax.ShapeDtypeStruct((Hq, W, 2 * W), F32),
                   jax.ShapeDtypeStruct((Hq, W, LANES), F32)],
        compiler_params=_params("arbitrary"),
        name="swa_bias",
    )(bucket, in_window, rel_bias, sinks)


def _swa_attn_kernel(q_ref, kc_ref, kp_ref, vc_ref, vp_ref, bias_ref, sink_ref, o_ref,
                     *, kv_heads, group):
    W = SWA_WINDOW
    i = pl.program_id(1)
    lane = lax.broadcasted_iota(jnp.int32, (1, LANES), 1)
    half0 = lane < SWA_HEAD
    rows = group * W
    col = lax.broadcasted_iota(jnp.int32, (rows, 2 * W), 1)
    key_ok = (col >= W) | (i > 0)
    scores = []
    for kv in range(kv_heads):
        ksl = slice(kv * LANES, (kv + 1) * LANES)
        k_band = jnp.concatenate([kp_ref[:, ksl], kc_ref[:, ksl]], 0)
        parts = []
        for pair in range(group // 2):
            blk = kv * (group // 2) + pair
            qp = q_ref[:, blk * LANES:(blk + 1) * LANES]
            zero = jnp.zeros_like(qp)
            parts += [jnp.where(half0, qp, zero), jnp.where(half0, zero, qp)]
        scores.append(_mm_nt(jnp.concatenate(parts, 0), k_band))
    for kv in range(kv_heads):
        ksl = slice(kv * LANES, (kv + 1) * LANES)
        v_band = jnp.concatenate([vp_ref[:, ksl], vc_ref[:, ksl]], 0)
        heads = slice(kv * group, (kv + 1) * group)
        s = scores[kv] + bias_ref[heads].reshape(rows, 2 * W)
        s = jnp.where(key_ok, s, NEG_INF)
        sink = sink_ref[heads].reshape(rows, LANES)
        m = jnp.maximum(jnp.max(s, -1, keepdims=True), sink)
        p = jnp.exp(s - jnp.concatenate([m] * (2 * W // LANES), -1))
        denom = jnp.sum(p, -1, keepdims=True) + jnp.exp(sink - m)
        o = jnp.dot(p.astype(MXU_DTYPE), v_band, preferred_element_type=F32) / denom
        for pair in range(group // 2):
            blk = kv * (group // 2) + pair
            o0 = o[(2 * pair) * W:(2 * pair + 1) * W]
            o1 = o[(2 * pair + 1) * W:(2 * pair + 2) * W]
            o_ref[:, blk * LANES:(blk + 1) * LANES] = jnp.where(half0, o0, o1).astype(o_ref.dtype)


def _swa_attention(q, kd, vd, bias, sink_rows, B, S, Hq, Hk):
    T = B * S
    W = SWA_WINDOW
    nb = S // W
    qw = Hq * SWA_HEAD
    kvw = Hk * LANES
    cur = lambda b, i: (b * nb + i, 0)
    prev = lambda b, i: (b * nb + jnp.maximum(i - 1, 0), 0)
    return pl.pallas_call(
        functools.partial(_swa_attn_kernel, kv_heads=Hk, group=Hq // Hk),
        grid=(B, nb),
        in_specs=[pl.BlockSpec((W, qw), cur),
                  pl.BlockSpec((W, kvw), cur), pl.BlockSpec((W, kvw), prev),
                  pl.BlockSpec((W, kvw), cur), pl.BlockSpec((W, kvw), prev),
                  _resident(bias.shape), _resident(sink_rows.shape)],
        out_specs=pl.BlockSpec((W, qw), cur),
        out_shape=jax.ShapeDtypeStruct((T, qw), MXU_DTYPE),
        compiler_params=_params("parallel", "arbitrary"),
        name="swa_attn",
    )(q, kd, kd, vd, vd, bias, sink_rows)


def _swa_layer(h, S, w_in, b_in, sinks, w_out, b_out, rel_bias, g, b, alpha):
    T, D = h.shape
    Hq = D // SWA_HEAD
    Hk = (w_in.shape[1] - D) // (2 * SWA_HEAD)
    assert (Hq // Hk) % 2 == 0
    q, kd, vd = _swa_qkv(h, w_in, b_in, Hq, Hk)
    bias, sink_rows = _swa_bias(rel_bias, sinks)
    o = _swa_attention(q, kd, vd, bias, sink_rows, T // S, S, Hq, Hk)
    return _proj_res_ln(o, w_out.astype(MXU_DTYPE), b_out, h, g, b, alpha)


def kernel(x, mla_w_in, mla_q_norm, mla_kv_norm, mla_w_q_b, mla_w_kv_b, mla_w_out, rwkv_mix, rwkv_w_in, rwkv_w0, rwkv_w1, rwkv_w2, rwkv_a0, rwkv_a1, rwkv_a2, rwkv_g1, rwkv_g2, rwkv_k_k, rwkv_k_a, rwkv_r_k, rwkv_ln_g, rwkv_ln_b, rwkv_w_out, swa_w_in, swa_b_in, swa_sinks, swa_w_out, swa_b_out, rel_bias, ln_g, ln_b, mlp_up, mlp_down):
    B, S, D = x.shape
    depth = ln_g.shape[0]
    alpha = (2.0 * depth) ** 0.25
    h = x.reshape(B * S, D)
    mlp_up_b = mlp_up.astype(MXU_DTYPE)
    mlp_down_b = mlp_down.astype(MXU_DTYPE)
    for i in range(depth):
        j = i // 3
        kind = i % 3
        g0, b0 = ln_g[i, 0], ln_b[i, 0]
        if kind == 0:
            h = _mla_layer(h, S, mla_w_in[j], mla_q_norm[j], mla_kv_norm[j], mla_w_q_b[j],
                           mla_w_kv_b[j], mla_w_out[j], g0, b0, alpha)
        elif kind == 1:
            h = _rwkv_layer(h, S, rwkv_mix[j], rwkv_w_in[j], rwkv_w0[j], rwkv_w1[j], rwkv_w2[j],
                            rwkv_a0[j], rwkv_a1[j], rwkv_a2[j], rwkv_g1[j], rwkv_g2[j],
                            rwkv_k_k[j], rwkv_k_a[j], rwkv_r_k[j].reshape(-1), rwkv_ln_g[j],
                            rwkv_ln_b[j], rwkv_w_out[j], g0, b0, alpha)
        else:
            h = _swa_layer(h, S, swa_w_in[j], swa_b_in[j], swa_sinks[j], swa_w_out[j],
                           swa_b_out[j], rel_bias, g0, b0, alpha)
        h = _mlp_res_ln(h, mlp_up_b, mlp_down_b, i, ln_g[i, 1], ln_b[i, 1], alpha)
    return h.reshape(B, S, D)
```
